```python
import jax, jax.numpy as jnp
from jax import lax
import numpy as np

D_MODEL = 1024
BATCH = 16
SEQ = 4096
DEPTH = 4

N_A = DEPTH // 2
N_B = DEPTH - N_A
N_HEADS = 8
HEAD_DIM = D_MODEL // (2 * N_HEADS)
V_DIM = 2 * HEAD_DIM
CONV_WIDTH = 3
N_GROUPS = 4
EXPERTS_PER_GROUP = 4
N_EXPERTS = N_GROUPS * EXPERTS_PER_GROUP
TOP_K = 2
D_EXPERT = D_MODEL // 2
Q_BLOCK = 128
EPS = 1e-6
NEG_INF = -1e30

kernel_name = "yoco_shortconv_diffattn_hmoe_adaln"


def rmsnorm(x, g):
    xf = x.astype(jnp.float32)
    y = xf * lax.rsqrt(jnp.mean(xf * xf, axis=-1, keepdims=True) + EPS)
    return (y * g.astype(jnp.float32)).astype(x.dtype)


def modulate(h, shift, scale):
    return h * (1 + scale[:, None, :]) + shift[:, None, :]


def alibi_slopes(n_heads):
    return jnp.exp2(-8.0 * (jnp.arange(n_heads, dtype=jnp.float32) + 1.0) / n_heads)


def short_conv_mixer(h, w_in, conv_w, conv_b, w_out):
    u = h @ w_in
    b_gate, c_gate, v = jnp.split(u, 3, axis=-1)
    z = c_gate * v
    kern = conv_w[:, None, :]
    zc = lax.conv_general_dilated(
        z, kern.astype(z.dtype), window_strides=(1,),
        padding=((CONV_WIDTH - 1, 0),),
        dimension_numbers=('NWC', 'WIO', 'NWC'),
        feature_group_count=D_MODEL) + conv_b
    return (b_gate * zc) @ w_out


def shared_kv(x, c, kv_mod_w, kv_mod_b, kv_norm_g, kv_w):
    bsz, seq, _ = x.shape
    m = jax.nn.silu(c) @ kv_mod_w + kv_mod_b
    shift, scale = jnp.split(m, 2, axis=-1)
    h = modulate(rmsnorm(x, kv_norm_g), shift, scale)
    kv = h @ kv_w
    k, v = jnp.split(kv, 2, axis=-1)
    k = k.reshape(bsz, seq, N_HEADS, 2, HEAD_DIM).transpose(3, 0, 2, 1, 4)
    v = v.reshape(bsz, seq, N_HEADS, V_DIM).transpose(0, 2, 1, 3)
    return k[0], k[1], v


def diff_attention(h, k1, k2, v, q_w, lq1, lk1, lq2, lk2, subln_g, o_w, lambda_init):
    bsz, seq, _ = h.shape
    nb = seq // Q_BLOCK
    q = (h @ q_w).reshape(bsz, nb, Q_BLOCK, N_HEADS, 2, HEAD_DIM)
    q = q.transpose(1, 4, 0, 3, 2, 5)
    lam = (jnp.exp(jnp.sum(lq1.astype(jnp.float32) * lk1.astype(jnp.float32)))
           - jnp.exp(jnp.sum(lq2.astype(jnp.float32) * lk2.astype(jnp.float32)))
           + lambda_init)
    slopes = alibi_slopes(N_HEADS)
    kpos = jnp.arange(seq)
    scale = HEAD_DIM ** -0.5

    def block(args):
        qb, start = args
        qpos = start + jnp.arange(Q_BLOCK)
        dist = qpos[:, None] - kpos[None, :]
        causal = dist >= 0
        bias = -slopes[:, None, None] * dist.astype(jnp.float32)[None]
        s1 = jnp.einsum('bhqd,bhkd->bhqk', qb[0], k1).astype(jnp.float32) * scale + bias
        s2 = jnp.einsum('bhqd,bhkd->bhqk', qb[1], k2).astype(jnp.float32) * scale + bias
        s1 = jnp.where(causal, s1, NEG_INF)
        s2 = jnp.where(causal, s2, NEG_INF)
        a = jax.nn.softmax(s1, axis=-1) - lam * jax.nn.softmax(s2, axis=-1)
        return jnp.einsum('bhqk,bhkv->bhqv', a.astype(v.dtype), v)

    starts = jnp.arange(nb) * Q_BLOCK
    o = lax.map(block, (q, starts))
    o = o.transpose(1, 0, 3, 2, 4).reshape(bsz, seq, N_HEADS, V_DIM)
    o = rmsnorm(o, subln_g) * (1.0 - lambda_init)
    return o.reshape(bsz, seq, N_HEADS * V_DIM) @ o_w


def hier_moe(h, gw, gb, ew, eb, w1, w3, w2):
    bsz, seq, d = h.shape
    t = h.reshape(-1, d)
    g_logits = (t @ gw + gb).astype(jnp.float32)
    p_g = jax.nn.softmax(g_logits, axis=-1)
    g_sel = jnp.argmax(g_logits, axis=-1)
    pg_sel = jnp.take_along_axis(p_g, g_sel[:, None], axis=-1)[:, 0]
    e_logits = (t @ ew + eb).astype(jnp.float32).reshape(-1, N_GROUPS, EXPERTS_PER_GROUP)
    e_sel_logits = jnp.take_along_axis(e_logits, g_sel[:, None, None], axis=1)[:, 0]
    p_e = jax.nn.softmax(e_sel_logits, axis=-1)
    top_v, top_i = lax.top_k(p_e, TOP_K)
    top_v = top_v / jnp.sum(top_v, axis=-1, keepdims=True)
    w_exp = jnp.sum(jax.nn.one_hot(top_i, EXPERTS_PER_GROUP, dtype=jnp.float32) * top_v[..., None], axis=1)
    combine = (jax.nn.one_hot(g_sel, N_GROUPS, dtype=jnp.float32)[:, :, None]
               * w_exp[:, None, :] * pg_sel[:, None, None]).reshape(-1, N_EXPERTS).astype(t.dtype)
    out = jnp.zeros_like(t)
    for e in range(N_EXPERTS):
        hid = jax.nn.silu(t @ w1[e]) * (t @ w3[e])
        out = out + combine[:, e:e + 1] * (hid @ w2[e])
    return out.reshape(bsz, seq, d)


def setup_inputs(seed: int = 0) -> dict:
    key = jax.random.key(seed)
    ks = iter(jax.random.split(key, 40))
    D = D_MODEL
    f32 = jnp.float32

    def nrm(shape, s):
        return jax.random.normal(next(ks), shape, f32) * s

    return {
        "x": nrm((BATCH, SEQ, D), 1.0),
        "c": nrm((BATCH, D), 1.0),
        "mod_w": nrm((DEPTH, D, 6 * D), 0.5 * D ** -0.5),
        "mod_b": nrm((DEPTH, 6 * D), 0.02),
        "norm_mix_g": 1.0 + nrm((DEPTH, D), 0.02),
        "norm_ffn_g": 1.0 + nrm((DEPTH, D), 0.02),
        "conv_in_w": nrm((N_A, D, 3 * D), D ** -0.5),
        "conv_w": nrm((N_A, CONV_WIDTH, D), CONV_WIDTH ** -0.5),
        "conv_b": nrm((N_A, D), 0.02),
        "conv_out_w": nrm((N_A, D, D), D ** -0.5),
        "kv_mod_w": nrm((D, 2 * D), 0.5 * D ** -0.5),
        "kv_mod_b": nrm((2 * D,), 0.02),
        "kv_norm_g": 1.0 + nrm((D,), 0.02),
        "kv_w": nrm((D, 2 * D), D ** -0.5),
        "q_w": nrm((N_B, D, D), D ** -0.5),
        "lam_q1": nrm((N_B, HEAD_DIM), 0.1),
        "lam_k1": nrm((N_B, HEAD_DIM), 0.1),
        "lam_q2": nrm((N_B, HEAD_DIM), 0.1),
        "lam_k2": nrm((N_B, HEAD_DIM), 0.1),
        "subln_g": 1.0 + nrm((N_B, V_DIM), 0.02),
        "o_w": nrm((N_B, D, D), D ** -0.5),
        "router_group_w": nrm((DEPTH, D, N_GROUPS), D ** -0.5),
        "router_group_b": nrm((DEPTH, N_GROUPS), 0.01),
        "router_exp_w": nrm((DEPTH, D, N_EXPERTS), D ** -0.5),
        "router_exp_b": nrm((DEPTH, N_EXPERTS), 0.01),
        "exp_w1": nrm((DEPTH, N_EXPERTS, D, D_EXPERT), D ** -0.5),
        "exp_w3": nrm((DEPTH, N_EXPERTS, D, D_EXPERT), D ** -0.5),
        "exp_w2": nrm((DEPTH, N_EXPERTS, D_EXPERT, D), D_EXPERT ** -0.5),
        "final_norm_g": 1.0 + nrm((D,), 0.02),
    }


def reference(x, c, mod_w, mod_b, norm_mix_g, norm_ffn_g, conv_in_w, conv_w, conv_b,
              conv_out_w, kv_mod_w, kv_mod_b, kv_norm_g, kv_w, q_w, lam_q1, lam_k1,
              lam_q2, lam_k2, subln_g, o_w, router_group_w, router_group_b,
              router_exp_w, router_exp_b, exp_w1, exp_w3, exp_w2, final_norm_g):
    c_act = jax.nn.silu(c)
    k1 = k2 = v = None
    for l in range(DEPTH):
        m = c_act @ mod_w[l] + mod_b[l]
        sh_a, sc_a, gt_a, sh_f, sc_f, gt_f = jnp.split(m, 6, axis=-1)
        if l == N_A:
            k1, k2, v = shared_kv(x, c, kv_mod_w, kv_mod_b, kv_norm_g, kv_w)
        h = modulate(rmsnorm(x, norm_mix_g[l]), sh_a, sc_a)
        if l < N_A:
            y = short_conv_mixer(h, conv_in_w[l], conv_w[l], conv_b[l], conv_out_w[l])
        else:
            j = l - N_A
            lambda_init = 0.8 - 0.6 * float(np.exp(-0.3 * l))
            y = diff_attention(h, k1, k2, v, q_w[j], lam_q1[j], lam_k1[j], lam_q2[j],
                               lam_k2[j], subln_g[j], o_w[j], lambda_init)
        x = x + gt_a[:, None, :] * y
        h = modulate(rmsnorm(x, norm_ffn_g[l]), sh_f, sc_f)
        y = hier_moe(h, router_group_w[l], router_group_b[l], router_exp_w[l],
                     router_exp_b[l], exp_w1[l], exp_w3[l], exp_w2[l])
        x = x + gt_f[:, None, :] * y
    return rmsnorm(x, final_norm_g)
```

```python
import functools

import jax
import jax.numpy as jnp
import numpy as np
from jax import lax
from jax.experimental import pallas as pl
from jax.experimental.pallas import tpu as pltpu

N_HEADS = 8
N_GROUPS = 4
EXPERTS_PER_GROUP = 4
N_EXPERTS = N_GROUPS * EXPERTS_PER_GROUP
CONV_WIDTH = 3
EPS = 1e-6
NEG_INF = -1e30
LANES = 128
SUBLANES = 8
VMEM_LIMIT = 48 * 1024 * 1024

F32 = jnp.float32
BF16 = jnp.bfloat16


def _params(sem):
    return pltpu.CompilerParams(dimension_semantics=sem, vmem_limit_bytes=VMEM_LIMIT)


def _norm_mod(x, g, sh, sc):
    ms = jnp.mean(x * x, axis=-1, keepdims=True)
    y = (x * lax.rsqrt(ms + EPS)) * g
    return y * (1.0 + sc) + sh


def _mod_kernel(c_ref, w_ref, b_ref, o_ref):
    c = c_ref[...]
    ca = c * jax.nn.sigmoid(c)
    o_ref[...] = jnp.dot(ca, w_ref[...], preferred_element_type=F32,
                         precision=lax.Precision.HIGHEST) + b_ref[...]


def _modulation(c, w, b, bn=1024):
    nl, d, n = w.shape
    bsz = c.shape[0]
    return pl.pallas_call(
        _mod_kernel,
        grid=(nl, n // bn),
        in_specs=[pl.BlockSpec((bsz, d), lambda l, j: (0, 0)),
                  pl.BlockSpec((None, d, bn), lambda l, j: (l, 0, j)),
                  pl.BlockSpec((None, 1, bn), lambda l, j: (l, 0, j))],
        out_specs=pl.BlockSpec((None, bsz, bn), lambda l, j: (l, 0, j)),
        out_shape=jax.ShapeDtypeStruct((nl, bsz, n), F32),
        compiler_params=_params(("arbitrary", "arbitrary")),
        name="modulation",
    )(c, w, b.reshape(nl, 1, n))


def _conv_kernel(x_ref, g_ref, sh_ref, sc_ref, gt_ref, win_ref, cw_ref, cb_ref, wout_ref,
                 o_ref, carry_ref, *, d, ts):
    @pl.when(pl.program_id(1) == 0)
    def _():
        carry_ref[...] = jnp.zeros_like(carry_ref)

    x = x_ref[...]
    h = _norm_mod(x, g_ref[...], sh_ref[...], sc_ref[...]).astype(BF16)
    c_gate = jnp.dot(h, win_ref[:, d:2 * d], preferred_element_type=F32)
    v = jnp.dot(h, win_ref[:, 2 * d:3 * d], preferred_element_type=F32)
    z = c_gate * v
    prev = carry_ref[...]
    row = lax.broadcasted_iota(jnp.int32, (ts, 1), 0)
    z1 = jnp.where(row == 0, prev[SUBLANES - 1:SUBLANES], pltpu.roll(z, 1, 0))
    z2 = jnp.where(row == 0, prev[SUBLANES - 2:SUBLANES - 1],
                   jnp.where(row == 1, prev[SUBLANES - 1:SUBLANES], pltpu.roll(z, 2, 0)))
    carry_ref[...] = z[ts - SUBLANES:ts]
    cw = cw_ref[...]
    zc = cw[0:1] * z2 + cw[1:2] * z1 + cw[2:3] * z + cb_ref[...]
    b_gate = jnp.dot(h, win_ref[:, 0:d], preferred_element_type=F32)
    y = jnp.dot((b_gate * zc).astype(BF16), wout_ref[...], preferred_element_type=F32)
    o_ref[...] = x + gt_ref[...] * y


def _conv_layer(x, mod, l, g, w_in, cw, cb, w_out, ts=512):
    bsz, seq, d = x.shape
    row = lambda i: pl.BlockSpec((None, 1, d), lambda b, s, i=i: ((l * bsz + b) * 6 + i, 0, 0))
    const2 = lambda shape: pl.BlockSpec(shape, lambda b, s: (0, 0))
    xspec = pl.BlockSpec((None, ts, d), lambda b, s: (b, s, 0))
    return pl.pallas_call(
        functools.partial(_conv_kernel, d=d, ts=ts),
        grid=(bsz, seq // ts),
        in_specs=[xspec, const2((1, d)), row(0), row(1), row(2), const2((d, 3 * d)),
                  const2((SUBLANES, d)), const2((1, d)), const2((d, d))],
        out_specs=xspec,
        out_shape=jax.ShapeDtypeStruct(x.shape, F32),
        scratch_shapes=[pltpu.VMEM((SUBLANES, d), F32)],
        compiler_params=_params(("arbitrary", "arbitrary")),
        name="conv_layer",
    )(x, g.reshape(1, d), mod, mod, mod, w_in, jnp.pad(cw, ((0, SUBLANES - CONV_WIDTH), (0, 0))),
      cb.reshape(1, d), w_out)


def _proj_kernel(x_ref, g_ref, sh_ref, sc_ref, w_ref, *o_refs, d, scale):
    h = _norm_mod(x_ref[...], g_ref[...], sh_ref[...], sc_ref[...]).astype(BF16)
    for j, o_ref in enumerate(o_refs):
        y = jnp.dot(h, w_ref[:, j * d:(j + 1) * d], preferred_element_type=F32)
        o_ref[...] = (y * scale).astype(o_ref.dtype)


def _proj(x, g, sh, sc, w, scale, ts=512):
    bsz, seq, d = x.shape
    n_out = w.shape[1] // d
    xspec = pl.BlockSpec((None, ts, d), lambda b, s: (b, s, 0))
    row = pl.BlockSpec((None, 1, d), lambda b, s: (b, 0, 0))
    return pl.pallas_call(
        functools.partial(_proj_kernel, d=d, scale=scale),
        grid=(bsz, seq // ts),
        in_specs=[xspec, pl.BlockSpec((1, d), lambda b, s: (0, 0)), row, row,
                  pl.BlockSpec(w.shape, lambda b, s: (0, 0))],
        out_specs=[xspec] * n_out,
        out_shape=[jax.ShapeDtypeStruct(x.shape, BF16)] * n_out,
        compiler_params=_params(("arbitrary", "arbitrary")),
        name="proj",
    )(x, g.reshape(1, d), sh, sc, w)


def _flash_kernel(slope_ref, q_ref, k_ref, v_ref, lq1_ref, lk1_ref, lq2_ref, lk2_ref, g_ref,
                  o_ref, qq_ref, m_ref, l_ref, acc_ref, *, tq, tk, hd, lambda_init):
    h = pl.program_id(1)
    qi = pl.program_id(2)
    slope = slope_ref[h]
    q = q_ref[...]
    lane = lax.broadcasted_iota(jnp.int32, q.shape, 1)
    zero = jnp.zeros_like(q)
    qq_ref[0:tq, :] = jnp.where(lane < hd, q, zero)
    qq_ref[tq:2 * tq, :] = jnp.where(lane >= hd, q, zero)
    m_ref[...] = jnp.full_like(m_ref, NEG_INF)
    l_ref[...] = jnp.zeros_like(l_ref)
    acc_ref[...] = jnp.zeros_like(acc_ref)
    r = lax.broadcasted_iota(jnp.int32, (2 * tq, tk), 0)
    c = lax.broadcasted_iota(jnp.int32, (2 * tq, tk), 1)
    local = jnp.where(r >= tq, r - tq, r) - c
    q0 = qi * tq

    def step(j, masked):
        k0 = j * tk
        kj = k_ref[pl.ds(pl.multiple_of(k0, tk), tk), :]
        vj = v_ref[pl.ds(pl.multiple_of(k0, tk), tk), :]
        s = lax.dot_general(qq_ref[...], kj, (((1,), (1,)), ((), ())), preferred_element_type=F32)
        dist = local + (q0 - k0)
        s = s - slope * dist.astype(F32)
        if masked:
            s = jnp.where(dist >= 0, s, NEG_INF)
        m_prev = m_ref[...]
        m_new = jnp.maximum(m_prev, jnp.max(s, axis=-1, keepdims=True))
        alpha = jnp.exp(m_prev - m_new)
        p = jnp.exp(s - m_new)
        l_ref[...] = alpha * l_ref[...] + jnp.sum(p, axis=-1, keepdims=True)
        acc_ref[...] = alpha * acc_ref[...] + jnp.dot(p.astype(BF16), vj, preferred_element_type=F32)
        m_ref[...] = m_new

    n_full = (q0 // tk)

    def body(j, carry):
        step(j, False)
        return carry

    lax.fori_loop(0, n_full, body, 0)
    n_diag = (q0 + tq + tk - 1) // tk - n_full
    for t in range((tq + tk - 1) // tk):
        @pl.when(t < n_diag)
        def _(t=t):
            step(n_full + t, True)

    lam = (jnp.exp(jnp.sum(lq1_ref[...] * lk1_ref[...], keepdims=True))
           - jnp.exp(jnp.sum(lq2_ref[...] * lk2_ref[...], keepdims=True)) + lambda_init)
    acc = acc_ref[...]
    l = l_ref[...]
    o = acc[0:tq] / l[0:tq] - lam * (acc[tq:2 * tq] / l[tq:2 * tq])
    on = o * lax.rsqrt(jnp.mean(o * o, axis=-1, keepdims=True) + EPS) * g_ref[...]
    o_ref[...] = (on * (1.0 - lambda_init)).astype(o_ref.dtype)


def _diff_attention(q, k, v, lq1, lk1, lq2, lk2, subln_g, lambda_init, tq=512, tk=512):
    bsz, seq, d = q.shape
    vd = d // N_HEADS
    hd = vd // 2
    slopes = jnp.exp2(-8.0 * (jnp.arange(N_HEADS, dtype=F32) + 1.0) / N_HEADS)
    vec = lambda n: pl.BlockSpec((1, n), lambda b, h, i, sl: (0, 0))
    kvspec = pl.BlockSpec((None, seq, vd), lambda b, h, i, sl: (b, 0, h))
    qspec = pl.BlockSpec((None, tq, vd), lambda b, h, i, sl: (b, i, h))
    grid_spec = pltpu.PrefetchScalarGridSpec(
        num_scalar_prefetch=1,
        grid=(bsz, N_HEADS, seq // tq),
        in_specs=[qspec, kvspec, kvspec, vec(hd), vec(hd), vec(hd), vec(hd), vec(vd)],
        out_specs=qspec,
        scratch_shapes=[pltpu.VMEM((2 * tq, vd), BF16), pltpu.VMEM((2 * tq, 1), F32),
                        pltpu.VMEM((2 * tq, 1), F32), pltpu.VMEM((2 * tq, vd), F32)],
    )
    return pl.pallas_call(
        functools.partial(_flash_kernel, tq=tq, tk=tk, hd=hd, lambda_init=lambda_init),
        grid_spec=grid_spec,
        out_shape=jax.ShapeDtypeStruct(q.shape, BF16),
        compiler_params=_params(("arbitrary", "arbitrary", "arbitrary")),
        name="diff_attention",
    )(slopes, q, k, v, lq1.reshape(1, hd), lk1.reshape(1, hd), lq2.reshape(1, hd), lk2.reshape(1, hd),
      subln_g.reshape(1, vd))


def _oproj_kernel(a_ref, w_ref, x_ref, gt_ref, o_ref):
    y = jnp.dot(a_ref[...], w_ref[...], preferred_element_type=F32)
    o_ref[...] = x_ref[...] + gt_ref[...] * y


def _oproj(a, w, x, mod, l, ts=512):
    bsz, seq, d = x.shape
    xspec = pl.BlockSpec((None, ts, d), lambda b, s: (b, s, 0))
    return pl.pallas_call(
        _oproj_kernel,
        grid=(bsz, seq // ts),
        in_specs=[xspec, pl.BlockSpec((d, d), lambda b, s: (0, 0)), xspec,
                  pl.BlockSpec((None, 1, d), lambda b, s: ((l * bsz + b) * 6 + 2, 0, 0))],
        out_specs=xspec,
        out_shape=jax.ShapeDtypeStruct(x.shape, F32),
        compiler_params=_params(("arbitrary", "arbitrary")),
        name="attn_out",
    )(a, w, x, mod)


def _router_kernel(x_ref, g_ref, sh_ref, sc_ref, rw_ref, rb_ref, h_ref, comb_ref):
    h = _norm_mod(x_ref[...], g_ref[...], sh_ref[...], sc_ref[...]).astype(BF16)
    h_ref[...] = h
    logits = jnp.dot(h, rw_ref[...], preferred_element_type=F32) + rb_ref[...]
    lane = lax.broadcasted_iota(jnp.int32, logits.shape, 1).astype(F32)
    big = float(LANES)
    gl = jnp.where(lane < N_GROUPS, logits, NEG_INF)
    mg = jnp.max(gl, axis=-1, keepdims=True)
    g_sel = jnp.min(jnp.where(gl == mg, lane, big), axis=-1, keepdims=True)
    pg_sel = 1.0 / jnp.sum(jnp.exp(gl - mg), axis=-1, keepdims=True)
    lo = N_GROUPS + float(EXPERTS_PER_GROUP) * g_sel
    el = jnp.where((lane >= lo) & (lane < lo + EXPERTS_PER_GROUP), logits, NEG_INF)
    m1 = jnp.max(el, axis=-1, keepdims=True)
    i1 = jnp.min(jnp.where(el == m1, lane, big), axis=-1, keepdims=True)
    el2 = jnp.where(lane == i1, NEG_INF, el)
    m2 = jnp.max(el2, axis=-1, keepdims=True)
    i2 = jnp.min(jnp.where(el2 == m2, lane, big), axis=-1, keepdims=True)
    e2 = jnp.exp(m2 - m1)
    w1 = 1.0 / (1.0 + e2)
    w2 = e2 / (1.0 + e2)
    comb_ref[...] = pg_sel * jnp.where(lane == i1, w1, jnp.where(lane == i2, w2, 0.0))


def _router(x, mod, l, g, rw, rb, tm=512):
    bsz, seq, d = x.shape
    x2 = x.reshape(bsz * seq, d)
    row = lambda i: pl.BlockSpec((None, 1, d), lambda t, i=i: ((l * bsz + (t * tm) // seq) * 6 + i, 0, 0))
    tile = lambda n: pl.BlockSpec((tm, n), lambda t: (t, 0))
    return pl.pallas_call(
        _router_kernel,
        grid=(bsz * seq // tm,),
        in_specs=[tile(d), pl.BlockSpec((1, d), lambda t: (0, 0)), row(3), row(4),
                  pl.BlockSpec((d, LANES), lambda t: (0, 0)), pl.BlockSpec((1, LANES), lambda t: (0, 0))],
        out_specs=[tile(d), tile(LANES)],
        out_shape=[jax.ShapeDtypeStruct((bsz * seq, d), BF16), jax.ShapeDtypeStruct((bsz * seq, LANES), F32)],
        compiler_params=_params(("arbitrary",)),
        name="router",
    )(x2, g.reshape(1, d), mod, mod, rw, rb)


def _moe_dense_kernel(h_ref, comb_ref, w1_ref, w3_ref, w2_ref, x_ref, gt_ref, fg_ref, o_ref, acc_ref,
                      *, final):
    e = pl.program_id(1)

    @pl.when(e == 0)
    def _():
        acc_ref[...] = jnp.zeros_like(acc_ref)

    h = h_ref[...]
    a = jnp.dot(h, w1_ref[...], preferred_element_type=F32)
    b = jnp.dot(h, w3_ref[...], preferred_element_type=F32)
    hid = (a * jax.nn.sigmoid(a)) * b
    y = jnp.dot(hid.astype(BF16), w2_ref[...], preferred_element_type=F32)
    comb = comb_ref[...]
    lane = lax.broadcasted_iota(jnp.int32, comb.shape, 1)
    ce = jnp.sum(jnp.where(lane == N_GROUPS + e, comb, 0.0), axis=-1, keepdims=True)
    acc_ref[...] += ce * y

    @pl.when(e == pl.num_programs(1) - 1)
    def _():
        xn = x_ref[...] + gt_ref[...] * acc_ref[...]
        if final:
            xn = (xn * lax.rsqrt(jnp.mean(xn * xn, axis=-1, keepdims=True) + EPS)) * fg_ref[...]
        o_ref[...] = xn


def _moe_dense(h, comb, w1, w3, w2, x, mod, l, final_g, final, tm=1024):
    bsz, seq, d = x.shape
    de = w1.shape[-1]
    x2 = x.reshape(bsz * seq, d)
    tile = lambda n: pl.BlockSpec((tm, n), lambda t, e: (t, 0))
    out = pl.pallas_call(
        functools.partial(_moe_dense_kernel, final=final),
        grid=(bsz * seq // tm, N_EXPERTS),
        in_specs=[tile(d), tile(LANES),
                  pl.BlockSpec((None, d, de), lambda t, e: (e, 0, 0)),
                  pl.BlockSpec((None, d, de), lambda t, e: (e, 0, 0)),
                  pl.BlockSpec((None, de, d), lambda t, e: (e, 0, 0)),
                  tile(d),
                  pl.BlockSpec((None, 1, d), lambda t, e: ((l * bsz + (t * tm) // seq) * 6 + 5, 0, 0)),
                  pl.BlockSpec((1, d), lambda t, e: (0, 0))],
        out_specs=tile(d),
        out_shape=jax.ShapeDtypeStruct((bsz * seq, d), F32),
        scratch_shapes=[pltpu.VMEM((tm, d), F32)],
        compiler_params=_params(("arbitrary", "arbitrary")),
        name="moe_dense",
    )(h, comb, w1, w3, w2, x2, mod, final_g.reshape(1, d))
    return out.reshape(bsz, seq, d)


def kernel(x, c, mod_w, mod_b, norm_mix_g, norm_ffn_g, conv_in_w, conv_w, conv_b, conv_out_w, kv_mod_w, kv_mod_b, kv_norm_g, kv_w, q_w, lam_q1, lam_k1, lam_q2, lam_k2, subln_g, o_w, router_group_w, router_group_b, router_exp_w, router_exp_b, exp_w1, exp_w3, exp_w2, final_norm_g):
    bsz, seq, d = x.shape
    depth = mod_w.shape[0]
    n_a = conv_in_w.shape[0]
    hd = d // (2 * N_HEADS)

    mod = _modulation(c, mod_w, mod_b).reshape(depth * bsz * 6, 1, d)
    kvm = _modulation(c, kv_mod_w[None], kv_mod_b[None]).reshape(bsz, 2, 1, d)
    kv_sh, kv_sc = kvm[:, 0], kvm[:, 1]

    pad = LANES - N_GROUPS - N_EXPERTS
    rw = jnp.pad(jnp.concatenate([router_group_w, router_exp_w], axis=-1), ((0, 0), (0, 0), (0, pad))).astype(BF16)
    rb = jnp.pad(jnp.concatenate([router_group_b, router_exp_b], axis=-1), ((0, 0), (0, pad)))[:, None, :]

    k = v = None
    for l in range(depth):
        if l == n_a:
            k, v = _proj(x, kv_norm_g, kv_sh, kv_sc, kv_w.astype(BF16), 1.0)
        if l < n_a:
            x = _conv_layer(x, mod, l, norm_mix_g[l], conv_in_w[l].astype(BF16), conv_w[l], conv_b[l],
                            conv_out_w[l].astype(BF16))
        else:
            j = l - n_a
            lambda_init = 0.8 - 0.6 * float(np.exp(-0.3 * l))
            m4 = mod.reshape(depth, bsz, 6, 1, d)
            (q,) = _proj(x, norm_mix_g[l], m4[l, :, 0], m4[l, :, 1], q_w[j].astype(BF16), hd ** -0.5)
            a = _diff_attention(q, k, v, lam_q1[j], lam_k1[j], lam_q2[j], lam_k2[j], subln_g[j], lambda_init)
            x = _oproj(a, o_w[j].astype(BF16), x, mod, l)
        h, comb = _router(x, mod, l, norm_ffn_g[l], rw[l], rb[l])
        x = _moe_dense(h, comb, exp_w1[l].astype(BF16), exp_w3[l].astype(BF16), exp_w2[l].astype(BF16),
                       x, mod, l, final_norm_g, final=(l == depth - 1))
    return x
```

```python
import functools

import jax
import jax.numpy as jnp
import numpy as np
from jax import lax
from jax.experimental import pallas as pl
from jax.experimental.pallas import tpu as pltpu

N_HEADS = 8
N_GROUPS = 4
EXPERTS_PER_GROUP = 4
N_EXPERTS = N_GROUPS * EXPERTS_PER_GROUP
CONV_WIDTH = 3
EPS = 1e-6
NEG_INF = -1e30
LANES = 128
SUBLANES = 8
POS_SPLIT = 128
VMEM_LIMIT = 48 * 1024 * 1024

F32 = jnp.float32
BF16 = jnp.bfloat16


def _params(sem):
    return pltpu.CompilerParams(dimension_semantics=sem, vmem_limit_bytes=VMEM_LIMIT)


def _norm_mod(x, g, sh, sc):
    ms = jnp.mean(x * x, axis=-1, keepdims=True)
    y = (x * lax.rsqrt(ms + EPS)) * g
    return y * (1.0 + sc) + sh


def _mod_kernel(c_ref, w_ref, b_ref, o_ref):
    c = c_ref[...]
    ca = c * jax.nn.sigmoid(c)
    o_ref[...] = jnp.dot(ca, w_ref[...], preferred_element_type=F32,
                         precision=lax.Precision.HIGHEST) + b_ref[...]


def _modulation(c, w, b, bn=1024):
    nl, d, n = w.shape
    bsz = c.shape[0]
    return pl.pallas_call(
        _mod_kernel,
        grid=(nl, n // bn),
        in_specs=[pl.BlockSpec((bsz, d), lambda l, j: (0, 0)),
                  pl.BlockSpec((None, d, bn), lambda l, j: (l, 0, j)),
                  pl.BlockSpec((None, 1, bn), lambda l, j: (l, 0, j))],
        out_specs=pl.BlockSpec((None, bsz, bn), lambda l, j: (l, 0, j)),
        out_shape=jax.ShapeDtypeStruct((nl, bsz, n), F32),
        compiler_params=_params(("arbitrary", "arbitrary")),
        name="modulation",
    )(c, w, b.reshape(nl, 1, n))


def _conv_kernel(x_ref, g_ref, sh_ref, sc_ref, gt_ref, win_ref, cw_ref, cb_ref, wout_ref,
                 o_ref, carry_ref, *, d, ts):
    @pl.when(pl.program_id(1) == 0)
    def _():
        carry_ref[...] = jnp.zeros_like(carry_ref)

    x = x_ref[...]
    h = _norm_mod(x, g_ref[...], sh_ref[...], sc_ref[...]).astype(BF16)
    c_gate = jnp.dot(h, win_ref[:, d:2 * d], preferred_element_type=F32)
    v = jnp.dot(h, win_ref[:, 2 * d:3 * d], preferred_element_type=F32)
    z = c_gate * v
    prev = carry_ref[...]
    row = lax.broadcasted_iota(jnp.int32, (ts, 1), 0)
    z1 = jnp.where(row == 0, prev[SUBLANES - 1:SUBLANES], pltpu.roll(z, 1, 0))
    z2 = jnp.where(row == 0, prev[SUBLANES - 2:SUBLANES - 1],
                   jnp.where(row == 1, prev[SUBLANES - 1:SUBLANES], pltpu.roll(z, 2, 0)))
    carry_ref[...] = z[ts - SUBLANES:ts]
    cw = cw_ref[...]
    zc = cw[0:1] * z2 + cw[1:2] * z1 + cw[2:3] * z + cb_ref[...]
    b_gate = jnp.dot(h, win_ref[:, 0:d], preferred_element_type=F32)
    y = jnp.dot((b_gate * zc).astype(BF16), wout_ref[...], preferred_element_type=F32)
    o_ref[...] = x + gt_ref[...] * y


def _conv_layer(x, mod, l, g, w_in, cw, cb, w_out, ts=512):
    bsz, seq, d = x.shape
    row = lambda i: pl.BlockSpec((None, 1, d), lambda b, s, i=i: ((l * bsz + b) * 6 + i, 0, 0))
    const2 = lambda shape: pl.BlockSpec(shape, lambda b, s: (0, 0))
    xspec = pl.BlockSpec((None, ts, d), lambda b, s: (b, s, 0))
    return pl.pallas_call(
        functools.partial(_conv_kernel, d=d, ts=ts),
        grid=(bsz, seq // ts),
        in_specs=[xspec, const2((1, d)), row(0), row(1), row(2), const2((d, 3 * d)),
                  const2((SUBLANES, d)), const2((1, d)), const2((d, d))],
        out_specs=xspec,
        out_shape=jax.ShapeDtypeStruct(x.shape, F32),
        scratch_shapes=[pltpu.VMEM((SUBLANES, d), F32)],
        compiler_params=_params(("arbitrary", "arbitrary")),
        name="conv_layer",
    )(x, g.reshape(1, d), mod, mod, mod, w_in, jnp.pad(cw, ((0, SUBLANES - CONV_WIDTH), (0, 0))),
      cb.reshape(1, d), w_out)


def _proj_kernel(x_ref, g_ref, sh_ref, sc_ref, w_ref, *o_refs, d, scale):
    h = _norm_mod(x_ref[...], g_ref[...], sh_ref[...], sc_ref[...]).astype(BF16)
    for j, o_ref in enumerate(o_refs):
        y = jnp.dot(h, w_ref[:, j * d:(j + 1) * d], preferred_element_type=F32)
        o_ref[...] = (y * scale).astype(o_ref.dtype)


def _proj(x, g, sh, sc, w, scale, ts=512):
    bsz, seq, d = x.shape
    n_out = w.shape[1] // d
    xspec = pl.BlockSpec((None, ts, d), lambda b, s: (b, s, 0))
    row = pl.BlockSpec((None, 1, d), lambda b, s: (b, 0, 0))
    return pl.pallas_call(
        functools.partial(_proj_kernel, d=d, scale=scale),
        grid=(bsz, seq // ts),
        in_specs=[xspec, pl.BlockSpec((1, d), lambda b, s: (0, 0)), row, row,
                  pl.BlockSpec(w.shape, lambda b, s: (0, 0))],
        out_specs=[xspec] * n_out,
        out_shape=[jax.ShapeDtypeStruct(x.shape, BF16)] * n_out,
        compiler_params=_params(("arbitrary", "arbitrary")),
        name="proj",
    )(x, g.reshape(1, d), sh, sc, w)


def _flash_kernel(slope_ref, q_ref, k_ref, v_ref, lq1_ref, lk1_ref, lq2_ref, lk2_ref, g_ref,
                  o_ref, qq_ref, kk_ref, vv_ref, s0_ref, s1_ref, m_ref, acc_ref, *, tq, tk, rc, hd, lambda_init):
    h = pl.program_id(1)
    qi = pl.program_id(2)
    seq, vd = k_ref.shape
    q0 = qi * tq

    @pl.when(qi == 0)
    def _():
        pos = lax.broadcasted_iota(jnp.int32, (seq, vd), 0)
        col = lax.broadcasted_iota(jnp.int32, (seq, vd), 1)
        lo = pos % POS_SPLIT
        kk_ref[:, 0:vd] = k_ref[...]
        kk_ref[:, vd:2 * vd] = jnp.where(col == 0, pos - lo, jnp.where(col == 1, lo, 0)).astype(F32).astype(BF16)
        vv_ref[:, 0:vd] = v_ref[...]
        vv_ref[:, vd:2 * vd] = jnp.ones((seq, vd), BF16)

    q = q_ref[...]
    lane = lax.broadcasted_iota(jnp.int32, q.shape, 1)
    zero = jnp.zeros_like(q)
    slope_cols = jnp.where(lane < 2, slope_ref[h], 0.0).astype(BF16)
    qq_ref[0:tq, 0:vd] = jnp.where(lane < hd, q, zero)
    qq_ref[tq:2 * tq, 0:vd] = jnp.where(lane >= hd, q, zero)
    qq_ref[0:tq, vd:2 * vd] = slope_cols
    qq_ref[tq:2 * tq, vd:2 * vd] = slope_cols
    m_ref[...] = jnp.full_like(m_ref, NEG_INF)
    acc_ref[...] = jnp.zeros_like(acc_ref)

    def scores(j, s_ref):
        kj = kk_ref[pl.ds(pl.multiple_of(j * tk, tk), tk), :]
        s_ref[...] = lax.dot_general(qq_ref[...], kj, (((1,), (1,)), ((), ())), preferred_element_type=F32)

    def update(j, s_ref, masked):
        vj = vv_ref[pl.ds(pl.multiple_of(j * tk, tk), tk), :]
        for c in range(2 * tq // rc):
            rows = pl.ds(c * rc, rc)
            s = s_ref[rows, :]
            if masked:
                qpos = lax.broadcasted_iota(jnp.int32, (rc, tk), 0) + (c * rc) % tq
                kpos = lax.broadcasted_iota(jnp.int32, (rc, tk), 1)
                s = jnp.where(qpos >= kpos, s, NEG_INF)
            m_prev = m_ref[rows, :]
            m_new = jnp.maximum(m_prev, jnp.max(s, axis=-1, keepdims=True))
            alpha = jnp.exp(m_prev - m_new)
            p = jnp.exp(s - jnp.tile(m_new, (1, tk // LANES)))
            pv = jnp.dot(p.astype(BF16), vj, preferred_element_type=F32)
            acc_ref[rows, :] = jnp.tile(alpha, (1, 2 * vd // LANES)) * acc_ref[rows, :] + pv
            m_ref[rows, :] = m_new

    scores(0, s0_ref)

    def body(i, carry):
        j = 2 * i
        scores(j + 1, s1_ref)
        update(j, s0_ref, False)
        scores(j + 2, s0_ref)
        update(j + 1, s1_ref, False)
        return carry

    lax.fori_loop(0, qi // 2, body, 0)

    @pl.when(qi % 2 == 0)
    def _():
        update(qi, s0_ref, True)

    @pl.when(qi % 2 == 1)
    def _():
        scores(qi, s1_ref)
        update(qi - 1, s0_ref, False)
        update(qi, s1_ref, True)

    lam = (jnp.exp(jnp.sum(lq1_ref[...] * lk1_ref[...], keepdims=True))
           - jnp.exp(jnp.sum(lq2_ref[...] * lk2_ref[...], keepdims=True)) + lambda_init)
    acc = acc_ref[...]
    o12 = acc[:, 0:vd] / acc[:, vd:2 * vd]
    o = o12[0:tq] - lam * o12[tq:2 * tq]
    on =o * lax.rsqrt(jnp.mean(o * o, axis=-1, keepdims=True) + EPS) * g_ref[...]
    o_ref[...] = (on * (1.0 - lambda_init)).astype(o_ref.dtype)


def _diff_attention(q, k, v, lq1, lk1, lq2, lk2, subln_g, lambda_init, tq=512, tk=512, rc=256):
    bsz, seq, d = q.shape
    vd = d // N_HEADS
    hd = vd // 2
    assert vd == LANES and seq <= POS_SPLIT * 256 and 8 % N_HEADS == 0 and tq == tk
    slopes = jnp.exp2(-8.0 * (jnp.arange(N_HEADS, dtype=F32) + 1.0) / N_HEADS)
    vec = lambda n: pl.BlockSpec((1, n), lambda b, h, i, sl: (0, 0))
    kvspec = pl.BlockSpec((None, seq, vd), lambda b, h, i, sl: (b, 0, h))
    qspec = pl.BlockSpec((None, tq, vd), lambda b, h, i, sl: (b, i, h))
    grid_spec = pltpu.PrefetchScalarGridSpec(
        num_scalar_prefetch=1,
        grid=(bsz, N_HEADS, seq // tq),
        in_specs=[qspec, kvspec, kvspec, vec(hd), vec(hd), vec(hd), vec(hd), vec(vd)],
        out_specs=qspec,
        scratch_shapes=[pltpu.VMEM((2 * tq, 2 * vd), BF16), pltpu.VMEM((seq, 2 * vd), BF16),
                        pltpu.VMEM((seq, 2 * vd), BF16), pltpu.VMEM((2 * tq, tk), F32), pltpu.VMEM((2 * tq, tk), F32),
                        pltpu.VMEM((2 * tq, LANES), F32),
                        pltpu.VMEM((2 * tq, 2 * vd), F32)],
    )
    return pl.pallas_call(
        functools.partial(_flash_kernel, tq=tq, tk=tk, rc=rc, hd=hd, lambda_init=lambda_init),
        grid_spec=grid_spec,
        out_shape=jax.ShapeDtypeStruct(q.shape, BF16),
        compiler_params=_params(("arbitrary", "arbitrary", "arbitrary")),
        name="diff_attention",
    )(slopes, q, k, v, lq1.reshape(1, hd), lk1.reshape(1, hd), lq2.reshape(1, hd), lk2.reshape(1, hd),
      subln_g.reshape(1, vd))


def _oproj_kernel(a_ref, w_ref, x_ref, gt_ref, o_ref):
    y = jnp.dot(a_ref[...], w_ref[...], preferred_element_type=F32)
    o_ref[...] = x_ref[...] + gt_ref[...] * y


def _oproj(a, w, x, mod, l, ts=512):
    bsz, seq, d = x.shape
    xspec = pl.BlockSpec((None, ts, d), lambda b, s: (b, s, 0))
    return pl.pallas_call(
        _oproj_kernel,
        grid=(bsz, seq // ts),
        in_specs=[xspec, pl.BlockSpec((d, d), lambda b, s: (0, 0)), xspec,
                  pl.BlockSpec((None, 1, d), lambda b, s: ((l * bsz + b) * 6 + 2, 0, 0))],
        out_specs=xspec,
        out_shape=jax.ShapeDtypeStruct(x.shape, F32),
        compiler_params=_params(("arbitrary", "arbitrary")),
        name="attn_out",
    )(a, w, x, mod)


PAIR_LO = (0, 0, 0, 1, 1, 2)
PAIR_HI = (1, 2, 3, 2, 3, 3)
N_PAIRS = len(PAIR_LO)
N_BUCKETS = N_GROUPS * N_PAIRS
DMA_UNROLL = 8


def _route_kernel(x_ref, g_ref, sh_ref, sc_ref, rw_ref, rb_ref, tri_ref, hlin_ref, meta_ref, cnt_ref, run_ref,
                  *, tm, nch):
    @pl.when(pl.program_id(0) == 0)
    def _():
        run_ref[...] = jnp.zeros_like(run_ref)

    h = _norm_mod(x_ref[...], g_ref[...], sh_ref[...], sc_ref[...]).astype(BF16)
    for j in range(nch):
        hlin_ref[pl.ds(j, tm, stride=nch), :] = h[:, j * LANES:(j + 1) * LANES].astype(F32)
    logits = jnp.dot(h, rw_ref[...], preferred_element_type=F32) + rb_ref[...]
    lane = lax.broadcasted_iota(jnp.int32, logits.shape, 1).astype(F32)
    big = float(LANES)
    gl = jnp.where(lane < N_GROUPS, logits, NEG_INF)
    mg = jnp.max(gl, axis=-1, keepdims=True)
    g_sel = jnp.min(jnp.where(gl == mg, lane, big), axis=-1, keepdims=True)
    lo = N_GROUPS + float(EXPERTS_PER_GROUP) * g_sel
    el = jnp.where((lane >= lo) & (lane < lo + EXPERTS_PER_GROUP), logits, NEG_INF)
    m1 = jnp.max(el, axis=-1, keepdims=True)
    i1 = jnp.min(jnp.where(el == m1, lane, big), axis=-1, keepdims=True)
    el2 = jnp.where(lane == i1, NEG_INF, el)
    m2 = jnp.max(el2, axis=-1, keepdims=True)
    i2 = jnp.min(jnp.where(el2 == m2, lane, big), axis=-1, keepdims=True)
    a = jnp.minimum(i1, i2) - lo
    b = jnp.maximum(i1, i2) - lo
    bucket = N_PAIRS * g_sel + (a * (7.0 - a) * 0.5 + b - a - 1.0)
    onehot = jnp.where(lane == bucket, 1.0, 0.0)
    before = jnp.dot(tri_ref[...], onehot.astype(BF16), preferred_element_type=F32) + run_ref[0:1, :]
    rank = jnp.sum(onehot * before, axis=-1, keepdims=True)
    meta_ref[...] = jnp.where(lane == 0.0, bucket, jnp.where(lane == 1.0, rank, 0.0))
    run = run_ref[...] + jnp.sum(onehot, axis=0, keepdims=True)
    run_ref[...] = run
    cnt_ref[...] = run


def _route(x2, mod, l, bsz, seq, g, rw, rb, tm=512):
    t_all, d = x2.shape
    nch = d // LANES
    row = lambda i: pl.BlockSpec((None, 1, d), lambda t, i=i: ((l * bsz + (t * tm) // seq) * 6 + i, 0, 0))
    tile = lambda n: pl.BlockSpec((tm, n), lambda t: (t, 0))
    const = lambda shape: pl.BlockSpec(shape, lambda t: (0, 0))
    tri = jnp.tril(jnp.ones((tm, tm), BF16), -1)
    return pl.pallas_call(
        functools.partial(_route_kernel, tm=tm, nch=nch),
        grid=(t_all // tm,),
        in_specs=[tile(d), const((1, d)), row(3), row(4), const((d, LANES)), const((1, LANES)), const((tm, tm))],
        out_specs=[pl.BlockSpec((tm * nch, LANES), lambda t: (t, 0)), tile(LANES), const((SUBLANES, LANES))],
        out_shape=[jax.ShapeDtypeStruct((t_all * nch, LANES), F32), jax.ShapeDtypeStruct((t_all, LANES), F32),
                   jax.ShapeDtypeStruct((SUBLANES, LANES), F32)],
        scratch_shapes=[pltpu.VMEM((SUBLANES, LANES), F32)],
        compiler_params=_params(("arbitrary",)),
        name="moe_route",
    )(x2, g.reshape(1, d), mod, mod, rw, rb, tri)


def _row_dma_loop(n_rows, start_fn):
    def body(i, carry):
        for u in range(DMA_UNROLL):
            start_fn(i * DMA_UNROLL + u)
        return carry
    lax.fori_loop(0, n_rows // DMA_UNROLL, body, 0)


def _dispatch_kernel(pos_ref, hlin_ref, hs_init_ref, hs_ref, sem, *, nch, chunk):
    del hs_init_ref

    def rows(ref, r, n=1):
        return ref.at[pl.ds(pl.multiple_of(r * nch, nch), n * nch)]

    def start(t):
        pltpu.make_async_copy(rows(hlin_ref, t), rows(hs_ref, pos_ref[t]), sem).start()

    def wait_chunk():
        pltpu.make_async_copy(rows(hlin_ref, 0, chunk), rows(hs_ref, 0, chunk), sem).wait()

    c = pl.program_id(0)
    _row_dma_loop(chunk, lambda r: start(c * chunk + r))

    @pl.when(c > 0)
    def _():
        wait_chunk()

    @pl.when(c == pl.num_programs(0) - 1)
    def _():
        wait_chunk()


def _dispatch(pos, hlin, n_rows, nch, chunk=1024):
    t_all = pos.shape[0]
    hs0 = jnp.zeros((n_rows * nch, LANES), F32)
    any_spec = pl.BlockSpec(memory_space=pl.ANY)
    grid_spec = pltpu.PrefetchScalarGridSpec(
        num_scalar_prefetch=1, grid=(t_all // chunk,), in_specs=[any_spec, any_spec], out_specs=any_spec,
        scratch_shapes=[pltpu.SemaphoreType.DMA(())])
    return pl.pallas_call(
        functools.partial(_dispatch_kernel, nch=nch, chunk=chunk),
        grid_spec=grid_spec,
        out_shape=jax.ShapeDtypeStruct(hs0.shape, F32),
        input_output_aliases={2: 0},
        compiler_params=_params(("arbitrary",)),
        name="moe_dispatch",
    )(pos, hlin, hs0)


def _ffn_kernel(ta_ref, tb_ref, tg_ref, tv_ref, tblk_ref, hs_ref, rw_ref, rb_ref,
                w1a_ref, w3a_ref, w2a_ref, w1b_ref, w3b_ref, w2b_ref, ys_ref, *, tm, nch):
    del tblk_ref
    i = pl.program_id(0)

    @pl.when(tv_ref[i] == 0)
    def _():
        ys_ref[...] = jnp.zeros_like(ys_ref)

    @pl.when(tv_ref[i] == 1)
    def _():
        x = jnp.concatenate([hs_ref[pl.ds(j, tm, stride=nch), :] for j in range(nch)], axis=-1).astype(BF16)
        logits = jnp.dot(x, rw_ref[...], preferred_element_type=F32) + rb_ref[...]
        lane = lax.broadcasted_iota(jnp.int32, logits.shape, 1)
        pick = lambda col, val: jnp.sum(jnp.where(lane == col, val, 0.0), axis=-1, keepdims=True)
        gl = jnp.where(lane < N_GROUPS, logits, NEG_INF)
        eg = jnp.exp(gl - jnp.max(gl, axis=-1, keepdims=True))
        pg = pick(tg_ref[i], eg) / jnp.sum(eg, axis=-1, keepdims=True)
        la = pick(N_GROUPS + ta_ref[i], logits)
        lb = pick(N_GROUPS + tb_ref[i], logits)
        mx = jnp.maximum(la, lb)
        ea = jnp.exp(la - mx)
        eb = jnp.exp(lb - mx)
        wa = pg * (ea / (ea + eb))
        wb = pg * (eb / (ea + eb))

        def ffn(w1_ref, w3_ref, w2_ref):
            a = jnp.dot(x, w1_ref[...], preferred_element_type=F32)
            b = jnp.dot(x, w3_ref[...], preferred_element_type=F32)
            hid = (a * jax.nn.sigmoid(a)) * b
            return jnp.dot(hid.astype(BF16), w2_ref[...], preferred_element_type=F32)

        y = wa * ffn(w1a_ref, w3a_ref, w2a_ref) + wb * ffn(w1b_ref, w3b_ref, w2b_ref)
        for j in range(nch):
            ys_ref[pl.ds(j, tm, stride=nch), :] = y[:, j * LANES:(j + 1) * LANES]


def _ffn(tiles, hs, rw, rb, w1, w3, w2, tm, nch):
    ta, tb, tg, tv, tblk = tiles
    n_tiles = ta.shape[0]
    d, de = w1.shape[1:]
    rows = pl.BlockSpec((tm * nch, LANES), lambda i, ta, tb, tg, tv, tblk: (tblk[i], 0))
    const = lambda shape: pl.BlockSpec(shape, lambda i, *_: (0, 0))
    wa = lambda shape: pl.BlockSpec((None,) + shape, lambda i, ta, tb, tg, tv, tblk: (ta[i], 0, 0))
    wb = lambda shape: pl.BlockSpec((None,) + shape, lambda i, ta, tb, tg, tv, tblk: (tb[i], 0, 0))
    grid_spec = pltpu.PrefetchScalarGridSpec(
        num_scalar_prefetch=5, grid=(n_tiles,),
        in_specs=[rows, const((d, LANES)), const((1, LANES)),
                  wa((d, de)), wa((d, de)), wa((de, d)), wb((d, de)), wb((d, de)), wb((de, d))],
        out_specs=pl.BlockSpec((tm * nch, LANES), lambda i, *_: (i, 0)))
    return pl.pallas_call(
        functools.partial(_ffn_kernel, tm=tm, nch=nch),
        grid_spec=grid_spec,
        out_shape=jax.ShapeDtypeStruct(hs.shape, F32),
        compiler_params=_params(("arbitrary",)),
        name="moe_ffn",
    )(ta, tb, tg, tv, tblk, hs, rw, rb, w1, w3, w2, w1, w3, w2)


def _combine_kernel(pos_ref, ys_ref, x_ref, gt_ref, fg_ref, o_ref, buf_ref, sem, *, tg, nch, final):
    i = pl.program_id(0)
    n = pl.num_programs(0)

    def rows(ref, r, k=1):
        return ref.at[pl.ds(pl.multiple_of(r * nch, nch), k * nch)]

    def fetch(tile, slot):
        def start(r):
            pltpu.make_async_copy(rows(ys_ref, pos_ref[tile * tg + r]), rows(buf_ref, slot * tg + r),
                                  sem.at[slot]).start()
        _row_dma_loop(tg, start)

    @pl.when(i == 0)
    def _():
        fetch(0, 0)

    @pl.when(i + 1 < n)
    def _():
        fetch(i + 1, (i + 1) % 2)

    slot = i % 2
    pltpu.make_async_copy(rows(ys_ref, 0, tg), rows(buf_ref, slot * tg, tg), sem.at[slot]).wait()
    base = slot * (tg * nch)
    y = jnp.concatenate([buf_ref[pl.ds(base + j, tg, stride=nch), :] for j in range(nch)], axis=-1)
    xn = x_ref[...] + gt_ref[...] * y
    if final:
        xn = (xn * lax.rsqrt(jnp.mean(xn * xn, axis=-1, keepdims=True) + EPS)) * fg_ref[...]
    o_ref[...] = xn


def _combine(pos, ys, x2, mod, l, bsz, seq, final_g, final, nch, tg=512):
    t_all, d = x2.shape
    tile = pl.BlockSpec((tg, d), lambda t, pos: (t, 0))
    grid_spec = pltpu.PrefetchScalarGridSpec(
        num_scalar_prefetch=1, grid=(t_all // tg,),
        in_specs=[pl.BlockSpec(memory_space=pl.ANY), tile,
                  pl.BlockSpec((None, 1, d), lambda t, pos: ((l * bsz + (t * tg) // seq) * 6 + 5, 0, 0)),
                  pl.BlockSpec((1, d), lambda t, pos: (0, 0))],
        out_specs=tile,
        scratch_shapes=[pltpu.VMEM((2 * tg * nch, LANES), F32), pltpu.SemaphoreType.DMA((2,))])
    return pl.pallas_call(
        functools.partial(_combine_kernel, tg=tg, nch=nch, final=final),
        grid_spec=grid_spec,
        out_shape=jax.ShapeDtypeStruct(x2.shape, F32),
        compiler_params=_params(("arbitrary",)),
        name="moe_combine",
    )(pos, ys, x2, mod, final_g.reshape(1, d))


def _moe(x, mod, l, g, rw, rb, w1, w3, w2, final_g, final, tm=512):
    bsz, seq, d = x.shape
    t_all = bsz * seq
    nch = d // LANES
    x2 = x.reshape(t_all, d)
    hlin, meta, cnt = _route(x2, mod, l, bsz, seq, g, rw, rb)

    n_tiles = t_all // tm + N_BUCKETS
    bucket = meta[:, 0].astype(jnp.int32)
    rank = meta[:, 1].astype(jnp.int32)
    count = cnt[0, :N_BUCKETS].astype(jnp.int32)
    padded = (count + tm - 1) // tm * tm
    ends = jnp.cumsum(padded)
    pos = (ends - padded)[bucket] + rank
    n_valid = ends[-1] // tm
    tile_id = jnp.arange(n_tiles, dtype=jnp.int32)
    tblk = jnp.minimum(tile_id, n_valid - 1)
    tbucket = jnp.minimum(jnp.searchsorted(ends, tblk * tm, side="right"), N_BUCKETS - 1).astype(jnp.int32)
    tgrp = tbucket // N_PAIRS
    ta = tgrp * EXPERTS_PER_GROUP + jnp.asarray(PAIR_LO, jnp.int32)[tbucket % N_PAIRS]
    tb = tgrp * EXPERTS_PER_GROUP + jnp.asarray(PAIR_HI, jnp.int32)[tbucket % N_PAIRS]
    tv = (tile_id < n_valid).astype(jnp.int32)

    hs = _dispatch(pos, hlin, n_tiles * tm, nch)
    ys = _ffn((ta, tb, tgrp, tv, tblk), hs, rw, rb, w1, w3, w2, tm, nch)
    out = _combine(pos, ys, x2, mod, l, bsz, seq, final_g, final, nch)
    return out.reshape(bsz, seq, d)


def kernel(x, c, mod_w, mod_b, norm_mix_g, norm_ffn_g, conv_in_w, conv_w, conv_b, conv_out_w, kv_mod_w, kv_mod_b, kv_norm_g, kv_w, q_w, lam_q1, lam_k1, lam_q2, lam_k2, subln_g, o_w, router_group_w, router_group_b, router_exp_w, router_exp_b, exp_w1, exp_w3, exp_w2, final_norm_g):
    bsz, seq, d = x.shape
    depth = mod_w.shape[0]
    n_a = conv_in_w.shape[0]
    hd = d // (2 * N_HEADS)

    mod = _modulation(c, mod_w, mod_b).reshape(depth * bsz * 6, 1, d)
    kvm = _modulation(c, kv_mod_w[None], kv_mod_b[None]).reshape(bsz, 2, 1, d)
    kv_sh, kv_sc = kvm[:, 0], kvm[:, 1]

    pad = LANES - N_GROUPS - N_EXPERTS
    rw = jnp.pad(jnp.concatenate([router_group_w, router_exp_w], axis=-1), ((0, 0), (0, 0), (0, pad))).astype(BF16)
    rb = jnp.pad(jnp.concatenate([router_group_b, router_exp_b], axis=-1), ((0, 0), (0, pad)))[:, None, :]

    k = v = None
    for l in range(depth):
        if l == n_a:
            k, v = _proj(x, kv_norm_g, kv_sh, kv_sc, kv_w.astype(BF16), 1.0)
        if l < n_a:
            x = _conv_layer(x, mod, l, norm_mix_g[l], conv_in_w[l].astype(BF16), conv_w[l], conv_b[l],
                            conv_out_w[l].astype(BF16))
        else:
            j = l - n_a
            lambda_init = 0.8 - 0.6 * float(np.exp(-0.3 * l))
            m4 = mod.reshape(depth, bsz, 6, 1, d)
            (q,) = _proj(x, norm_mix_g[l], m4[l, :, 0], m4[l, :, 1], q_w[j].astype(BF16), hd ** -0.5)
            a = _diff_attention(q, k, v, lam_q1[j], lam_k1[j], lam_q2[j], lam_k2[j], subln_g[j], lambda_init)
            x = _oproj(a, o_w[j].astype(BF16), x, mod, l)
        x = _moe(x, mod, l, norm_ffn_g[l], rw[l], rb[l], exp_w1[l].astype(BF16), exp_w3[l].astype(BF16),
                 exp_w2[l].astype(BF16), final_norm_g, final=(l == depth - 1))
    return x
```

```python
import functools

import jax
import jax.numpy as jnp
import numpy as np
from jax import lax
from jax.experimental import pallas as pl
from jax.experimental.pallas import tpu as pltpu

N_HEADS = 8
N_GROUPS = 4
EXPERTS_PER_GROUP = 4
N_EXPERTS = N_GROUPS * EXPERTS_PER_GROUP
CONV_WIDTH = 3
EPS = 1e-6
NEG_INF = -1e30
LANES = 128
SUBLANES = 8
POS_SPLIT = 128
VMEM_LIMIT = 48 * 1024 * 1024

F32 = jnp.float32
BF16 = jnp.bfloat16


def _params(sem):
    return pltpu.CompilerParams(dimension_semantics=sem, vmem_limit_bytes=VMEM_LIMIT)


def _norm_mod(x, g, sh, sc):
    ms = jnp.mean(x * x, axis=-1, keepdims=True)
    y = (x * lax.rsqrt(ms + EPS)) * g
    return y * (1.0 + sc) + sh


def _mod_kernel(c_ref, w_ref, b_ref, o_ref):
    c = c_ref[...]
    ca = c * jax.nn.sigmoid(c)
    o_ref[...] = jnp.dot(ca, w_ref[...], preferred_element_type=F32,
                         precision=lax.Precision.HIGHEST) + b_ref[...]


def _modulation(c, w, b, bn=1024):
    nl, d, n = w.shape
    bsz = c.shape[0]
    return pl.pallas_call(
        _mod_kernel,
        grid=(nl, n // bn),
        in_specs=[pl.BlockSpec((bsz, d), lambda l, j: (0, 0)),
                  pl.BlockSpec((None, d, bn), lambda l, j: (l, 0, j)),
                  pl.BlockSpec((None, 1, bn), lambda l, j: (l, 0, j))],
        out_specs=pl.BlockSpec((None, bsz, bn), lambda l, j: (l, 0, j)),
        out_shape=jax.ShapeDtypeStruct((nl, bsz, n), F32),
        compiler_params=_params(("arbitrary", "arbitrary")),
        name="modulation",
    )(c, w, b.reshape(nl, 1, n))


def _conv_kernel(x_ref, g_ref, sh_ref, sc_ref, gt_ref, win_ref, cw_ref, cb_ref, wout_ref,
                 o_ref, carry_ref, *, d, ts):
    @pl.when(pl.program_id(1) == 0)
    def _():
        carry_ref[...] = jnp.zeros_like(carry_ref)

    x = x_ref[...]
    h = _norm_mod(x, g_ref[...], sh_ref[...], sc_ref[...]).astype(BF16)
    c_gate = jnp.dot(h, win_ref[:, d:2 * d], preferred_element_type=F32)
    v = jnp.dot(h, win_ref[:, 2 * d:3 * d], preferred_element_type=F32)
    z = c_gate * v
    prev = carry_ref[...]
    row = lax.broadcasted_iota(jnp.int32, (ts, 1), 0)
    z1 = jnp.where(row == 0, prev[SUBLANES - 1:SUBLANES], pltpu.roll(z, 1, 0))
    z2 = jnp.where(row == 0, prev[SUBLANES - 2:SUBLANES - 1],
                   jnp.where(row == 1, prev[SUBLANES - 1:SUBLANES], pltpu.roll(z, 2, 0)))
    carry_ref[...] = z[ts - SUBLANES:ts]
    cw = cw_ref[...]
    zc = cw[0:1] * z2 + cw[1:2] * z1 + cw[2:3] * z + cb_ref[...]
    b_gate = jnp.dot(h, win_ref[:, 0:d], preferred_element_type=F32)
    y = jnp.dot((b_gate * zc).astype(BF16), wout_ref[...], preferred_element_type=F32)
    o_ref[...] = x + gt_ref[...] * y


def _conv_layer(x, mod, l, g, w_in, cw, cb, w_out, ts=512):
    bsz, seq, d = x.shape
    row = lambda i: pl.BlockSpec((None, 1, d), lambda b, s, i=i: ((l * bsz + b) * 6 + i, 0, 0))
    const2 = lambda shape: pl.BlockSpec(shape, lambda b, s: (0, 0))
    xspec = pl.BlockSpec((None, ts, d), lambda b, s: (b, s, 0))
    return pl.pallas_call(
        functools.partial(_conv_kernel, d=d, ts=ts),
        grid=(bsz, seq // ts),
        in_specs=[xspec, const2((1, d)), row(0), row(1), row(2), const2((d, 3 * d)),
                  const2((SUBLANES, d)), const2((1, d)), const2((d, d))],
        out_specs=xspec,
        out_shape=jax.ShapeDtypeStruct(x.shape, F32),
        scratch_shapes=[pltpu.VMEM((SUBLANES, d), F32)],
        compiler_params=_params(("arbitrary", "arbitrary")),
        name="conv_layer",
    )(x, g.reshape(1, d), mod, mod, mod, w_in, jnp.pad(cw, ((0, SUBLANES - CONV_WIDTH), (0, 0))),
      cb.reshape(1, d), w_out)


def _proj_kernel(x_ref, g_ref, sh_ref, sc_ref, w_ref, *o_refs, d, scale):
    h = _norm_mod(x_ref[...], g_ref[...], sh_ref[...], sc_ref[...]).astype(BF16)
    for j, o_ref in enumerate(o_refs):
        y = jnp.dot(h, w_ref[:, j * d:(j + 1) * d], preferred_element_type=F32)
        o_ref[...] = (y * scale).astype(o_ref.dtype)


def _proj(x, g, sh, sc, w, scale, ts=512):
    bsz, seq, d = x.shape
    n_out = w.shape[1] // d
    xspec = pl.BlockSpec((None, ts, d), lambda b, s: (b, s, 0))
    row = pl.BlockSpec((None, 1, d), lambda b, s: (b, 0, 0))
    return pl.pallas_call(
        functools.partial(_proj_kernel, d=d, scale=scale),
        grid=(bsz, seq // ts),
        in_specs=[xspec, pl.BlockSpec((1, d), lambda b, s: (0, 0)), row, row,
                  pl.BlockSpec(w.shape, lambda b, s: (0, 0))],
        out_specs=[xspec] * n_out,
        out_shape=[jax.ShapeDtypeStruct(x.shape, BF16)] * n_out,
        compiler_params=_params(("arbitrary", "arbitrary")),
        name="proj",
    )(x, g.reshape(1, d), sh, sc, w)


def _flash_kernel(slope_ref, q_ref, k_ref, v_ref, lq1_ref, lk1_ref, lq2_ref, lk2_ref, g_ref,
                  o_ref, qq_ref, kk_ref, vv_ref, s0_ref, s1_ref, m_ref, acc_ref, *, tq, tk, rc, hd, lambda_init):
    h = pl.program_id(1)
    qi = pl.program_id(2)
    seq, vd = k_ref.shape
    q0 = qi * tq

    @pl.when(qi == 0)
    def _():
        pos = lax.broadcasted_iota(jnp.int32, (seq, vd), 0)
        col = lax.broadcasted_iota(jnp.int32, (seq, vd), 1)
        lo = pos % POS_SPLIT
        kk_ref[:, 0:vd] = k_ref[...]
        kk_ref[:, vd:2 * vd] = jnp.where(col == 0, pos - lo, jnp.where(col == 1, lo, 0)).astype(F32).astype(BF16)
        vv_ref[:, 0:vd] = v_ref[...]
        vv_ref[:, vd:2 * vd] = jnp.ones((seq, vd), BF16)

    q = q_ref[...]
    lane = lax.broadcasted_iota(jnp.int32, q.shape, 1)
    zero = jnp.zeros_like(q)
    slope_cols = jnp.where(lane < 2, slope_ref[h], 0.0).astype(BF16)
    qq_ref[0:tq, 0:vd] = jnp.where(lane < hd, q, zero)
    qq_ref[tq:2 * tq, 0:vd] = jnp.where(lane >= hd, q, zero)
    qq_ref[0:tq, vd:2 * vd] = slope_cols
    qq_ref[tq:2 * tq, vd:2 * vd] = slope_cols
    m_ref[...] = jnp.full_like(m_ref, NEG_INF)
    acc_ref[...] = jnp.zeros_like(acc_ref)

    def scores(j, s_ref):
        kj = kk_ref[pl.ds(pl.multiple_of(j * tk, tk), tk), :]
        s_ref[...] = lax.dot_general(qq_ref[...], kj, (((1,), (1,)), ((), ())), preferred_element_type=F32)

    def update(j, s_ref, masked):
        vj = vv_ref[pl.ds(pl.multiple_of(j * tk, tk), tk), :]
        for c in range(2 * tq // rc):
            rows = pl.ds(c * rc, rc)
            s = s_ref[rows, :]
            if masked:
                qpos = lax.broadcasted_iota(jnp.int32, (rc, tk), 0) + (c * rc) % tq
                kpos = lax.broadcasted_iota(jnp.int32, (rc, tk), 1)
                s = jnp.where(qpos >= kpos, s, NEG_INF)
            m_prev = m_ref[rows, :]
            m_new = jnp.maximum(m_prev, jnp.max(s, axis=-1, keepdims=True))
            alpha = jnp.exp(m_prev - m_new)
            p = jnp.exp(s - jnp.tile(m_new, (1, tk // LANES)))
            pv = jnp.dot(p.astype(BF16), vj, preferred_element_type=F32)
            acc_ref[rows, :] = jnp.tile(alpha, (1, 2 * vd // LANES)) * acc_ref[rows, :] + pv
            m_ref[rows, :] = m_new

    scores(0, s0_ref)

    def body(i, carry):
        j = 2 * i
        scores(j + 1, s1_ref)
        update(j, s0_ref, False)
        scores(j + 2, s0_ref)
        update(j + 1, s1_ref, False)
        return carry

    lax.fori_loop(0, qi // 2, body, 0)

    @pl.when(qi % 2 == 0)
    def _():
        update(qi, s0_ref, True)

    @pl.when(qi % 2 == 1)
    def _():
        scores(qi, s1_ref)
        update(qi - 1, s0_ref, False)
        update(qi, s1_ref, True)

    lam = (jnp.exp(jnp.sum(lq1_ref[...] * lk1_ref[...], keepdims=True))
           - jnp.exp(jnp.sum(lq2_ref[...] * lk2_ref[...], keepdims=True)) + lambda_init)
    acc = acc_ref[...]
    o12 = acc[:, 0:vd] / acc[:, vd:2 * vd]
    o = o12[0:tq] - lam * o12[tq:2 * tq]
    on =o * lax.rsqrt(jnp.mean(o * o, axis=-1, keepdims=True) + EPS) * g_ref[...]
    o_ref[...] = (on * (1.0 - lambda_init)).astype(o_ref.dtype)


def _diff_attention(q, k, v, lq1, lk1, lq2, lk2, subln_g, lambda_init, tq=512, tk=512, rc=256):
    bsz, seq, d = q.shape
    vd = d // N_HEADS
    hd = vd // 2
    assert vd == LANES and seq <= POS_SPLIT * 256 and 8 % N_HEADS == 0 and tq == tk
    slopes = jnp.exp2(-8.0 * (jnp.arange(N_HEADS, dtype=F32) + 1.0) / N_HEADS)
    vec = lambda n: pl.BlockSpec((1, n), lambda b, h, i, sl: (0, 0))
    kvspec = pl.BlockSpec((None, seq, vd), lambda b, h, i, sl: (b, 0, h))
    qspec = pl.BlockSpec((None, tq, vd), lambda b, h, i, sl: (b, i, h))
    grid_spec = pltpu.PrefetchScalarGridSpec(
        num_scalar_prefetch=1,
        grid=(bsz, N_HEADS, seq // tq),
        in_specs=[qspec, kvspec, kvspec, vec(hd), vec(hd), vec(hd), vec(hd), vec(vd)],
        out_specs=qspec,
        scratch_shapes=[pltpu.VMEM((2 * tq, 2 * vd), BF16), pltpu.VMEM((seq, 2 * vd), BF16),
                        pltpu.VMEM((seq, 2 * vd), BF16), pltpu.VMEM((2 * tq, tk), F32), pltpu.VMEM((2 * tq, tk), F32),
                        pltpu.VMEM((2 * tq, LANES), F32),
                        pltpu.VMEM((2 * tq, 2 * vd), F32)],
    )
    return pl.pallas_call(
        functools.partial(_flash_kernel, tq=tq, tk=tk, rc=rc, hd=hd, lambda_init=lambda_init),
        grid_spec=grid_spec,
        out_shape=jax.ShapeDtypeStruct(q.shape, BF16),
        compiler_params=_params(("arbitrary", "arbitrary", "arbitrary")),
        name="diff_attention",
    )(slopes, q, k, v, lq1.reshape(1, hd), lk1.reshape(1, hd), lq2.reshape(1, hd), lk2.reshape(1, hd),
      subln_g.reshape(1, vd))


def _oproj_kernel(a_ref, w_ref, x_ref, gt_ref, o_ref):
    y = jnp.dot(a_ref[...], w_ref[...], preferred_element_type=F32)
    o_ref[...] = x_ref[...] + gt_ref[...] * y


def _oproj(a, w, x, mod, l, ts=512):
    bsz, seq, d = x.shape
    xspec = pl.BlockSpec((None, ts, d), lambda b, s: (b, s, 0))
    return pl.pallas_call(
        _oproj_kernel,
        grid=(bsz, seq // ts),
        in_specs=[xspec, pl.BlockSpec((d, d), lambda b, s: (0, 0)), xspec,
                  pl.BlockSpec((None, 1, d), lambda b, s: ((l * bsz + b) * 6 + 2, 0, 0))],
        out_specs=xspec,
        out_shape=jax.ShapeDtypeStruct(x.shape, F32),
        compiler_params=_params(("arbitrary", "arbitrary")),
        name="attn_out",
    )(a, w, x, mod)


PAIR_LO = (0, 0, 0, 1, 1, 2)
PAIR_HI = (1, 2, 3, 2, 3, 3)
N_PAIRS = len(PAIR_LO)
N_BUCKETS = N_GROUPS * N_PAIRS
DMA_UNROLL = 8
RANK_SPLIT = 256


def _route_kernel(x_ref, g_ref, sh_ref, sc_ref, rw_ref, rb_ref, tri_ref, hlin_ref, meta_ref, cnt_ref, run_ref,
                  *, tm, nch):
    @pl.when(pl.program_id(0) == 0)
    def _():
        run_ref[...] = jnp.zeros_like(run_ref)

    h = _norm_mod(x_ref[...], g_ref[...], sh_ref[...], sc_ref[...]).astype(BF16)
    for j in range(nch):
        hlin_ref[pl.ds(j, tm, stride=nch), :] = h[:, j * LANES:(j + 1) * LANES].astype(F32)
    logits = jnp.dot(h, rw_ref[...], preferred_element_type=F32) + rb_ref[...]
    lane = lax.broadcasted_iota(jnp.int32, logits.shape, 1).astype(F32)
    big = float(LANES)
    gl = jnp.where(lane < N_GROUPS, logits, NEG_INF)
    mg = jnp.max(gl, axis=-1, keepdims=True)
    g_sel = jnp.min(jnp.where(gl == mg, lane, big), axis=-1, keepdims=True)
    lo = N_GROUPS + float(EXPERTS_PER_GROUP) * g_sel
    el = jnp.where((lane >= lo) & (lane < lo + EXPERTS_PER_GROUP), logits, NEG_INF)
    m1 = jnp.max(el, axis=-1, keepdims=True)
    i1 = jnp.min(jnp.where(el == m1, lane, big), axis=-1, keepdims=True)
    el2 = jnp.where(lane == i1, NEG_INF, el)
    m2 = jnp.max(el2, axis=-1, keepdims=True)
    i2 = jnp.min(jnp.where(el2 == m2, lane, big), axis=-1, keepdims=True)
    a = jnp.minimum(i1, i2) - lo
    b = jnp.maximum(i1, i2) - lo
    bucket = N_PAIRS * g_sel + (a * (7.0 - a) * 0.5 + b - a - 1.0)
    onehot = jnp.where(lane == bucket, 1.0, 0.0)
    before = jnp.dot(tri_ref[...], onehot.astype(BF16), preferred_element_type=F32) + run_ref[0:1, :]
    rank = jnp.sum(onehot * before, axis=-1, keepdims=True)
    rank_hi = jnp.floor(rank * (1.0 / RANK_SPLIT))
    cols = jnp.where(lane == 0.0, bucket, jnp.where(lane == 1.0, rank_hi,
                                                    jnp.where(lane == 2.0, rank - RANK_SPLIT * rank_hi, 0.0)))
    sel = (lax.broadcasted_iota(jnp.int32, (SUBLANES, LANES), 0)
           == lax.broadcasted_iota(jnp.int32, (SUBLANES, LANES), 1))
    meta_ref[...] = lax.dot_general(jnp.where(sel, 1.0, 0.0).astype(BF16), cols.astype(BF16),
                                    (((1,), (1,)), ((), ())), preferred_element_type=F32)
    run = run_ref[...] + jnp.sum(onehot, axis=0, keepdims=True)
    run_ref[...] = run
    cnt_ref[...] = run


def _route(x2, mod, l, bsz, seq, g, rw, rb, tm=512):
    t_all, d = x2.shape
    nch = d // LANES
    assert t_all <= RANK_SPLIT * RANK_SPLIT
    row = lambda i: pl.BlockSpec((None, 1, d), lambda t, i=i: ((l * bsz + (t * tm) // seq) * 6 + i, 0, 0))
    tile = lambda n: pl.BlockSpec((tm, n), lambda t: (t, 0))
    const = lambda shape: pl.BlockSpec(shape, lambda t: (0, 0))
    tri = jnp.tril(jnp.ones((tm, tm), BF16), -1)
    return pl.pallas_call(
        functools.partial(_route_kernel, tm=tm, nch=nch),
        grid=(t_all // tm,),
        in_specs=[tile(d), const((1, d)), row(3), row(4), const((d, LANES)), const((1, LANES)), const((tm, tm))],
        out_specs=[pl.BlockSpec((tm * nch, LANES), lambda t: (t, 0)), pl.BlockSpec((SUBLANES, tm), lambda t: (0, t)),
                   const((SUBLANES, LANES))],
        out_shape=[jax.ShapeDtypeStruct((t_all * nch, LANES), F32), jax.ShapeDtypeStruct((SUBLANES, t_all), F32),
                   jax.ShapeDtypeStruct((SUBLANES, LANES), F32)],
        scratch_shapes=[pltpu.VMEM((SUBLANES, LANES), F32)],
        compiler_params=_params(("arbitrary",)),
        name="moe_route",
    )(x2, g.reshape(1, d), mod, mod, rw, rb, tri)


def _row_dma_loop(n_rows, start_fn):
    def body(i, carry):
        for u in range(DMA_UNROLL):
            start_fn(i * DMA_UNROLL + u)
        return carry
    lax.fori_loop(0, n_rows // DMA_UNROLL, body, 0)


def _dispatch_kernel(pos_ref, ztile_ref, hlin_ref, hs_ref, zero_ref, sem, zsem, *, nch, chunk, tm):
    def rows(ref, r, n=1):
        return ref.at[pl.ds(pl.multiple_of(r * nch, nch), n * nch)]

    c = pl.program_id(0)

    @pl.when(c == 0)
    def _():
        zero_ref[...] = jnp.zeros_like(zero_ref)
        fill = lambda k: pltpu.make_async_copy(zero_ref, rows(hs_ref, ztile_ref[k] * tm, tm), zsem)
        for k in range(ztile_ref.shape[0]):
            @pl.when(ztile_ref[k] >= 0)
            def _(k=k):
                fill(k).start()
        for k in range(ztile_ref.shape[0]):
            @pl.when(ztile_ref[k] >= 0)
            def _(k=k):
                fill(k).wait()

    _row_dma_loop(chunk, lambda r: pltpu.make_async_copy(
        rows(hlin_ref, r), rows(hs_ref, pos_ref[c * chunk + r]), sem).start())
    pltpu.make_async_copy(hlin_ref, rows(hs_ref, 0, chunk), sem).wait()


def _dispatch(pos, ztile, hlin, n_rows, nch, tm, chunk=1024):
    t_all = pos.shape[0]
    grid_spec = pltpu.PrefetchScalarGridSpec(
        num_scalar_prefetch=2, grid=(t_all // chunk,),
        in_specs=[pl.BlockSpec((chunk * nch, LANES), lambda c, pos, zt: (c, 0))],
        out_specs=pl.BlockSpec(memory_space=pl.ANY),
        scratch_shapes=[pltpu.VMEM((tm * nch, LANES), F32), pltpu.SemaphoreType.DMA(()),
                        pltpu.SemaphoreType.DMA(())])
    return pl.pallas_call(
        functools.partial(_dispatch_kernel, nch=nch, chunk=chunk, tm=tm),
        grid_spec=grid_spec,
        out_shape=jax.ShapeDtypeStruct((n_rows * nch, LANES), F32),
        compiler_params=_params(("arbitrary",)),
        name="moe_dispatch",
    )(pos, ztile, hlin)


def _ffn_kernel(ta_ref, tb_ref, tg_ref, tv_ref, tblk_ref, hs_ref, rw_ref, rb_ref,
                w1a_ref, w3a_ref, w2a_ref, w1b_ref, w3b_ref, w2b_ref, ys_ref, *, tm, nch):
    del tblk_ref
    i = pl.program_id(0)

    @pl.when(tv_ref[i] == 0)
    def _():
        ys_ref[...] = jnp.zeros_like(ys_ref)

    @pl.when(tv_ref[i] == 1)
    def _():
        x = jnp.concatenate([hs_ref[pl.ds(j, tm, stride=nch), :] for j in range(nch)], axis=-1).astype(BF16)
        logits = jnp.dot(x, rw_ref[...], preferred_element_type=F32) + rb_ref[...]
        lane = lax.broadcasted_iota(jnp.int32, logits.shape, 1)
        pick = lambda col, val: jnp.sum(jnp.where(lane == col, val, 0.0), axis=-1, keepdims=True)
        gl = jnp.where(lane < N_GROUPS, logits, NEG_INF)
        eg = jnp.exp(gl - jnp.max(gl, axis=-1, keepdims=True))
        pg = pick(tg_ref[i], eg) / jnp.sum(eg, axis=-1, keepdims=True)
        la = pick(N_GROUPS + ta_ref[i], logits)
        lb = pick(N_GROUPS + tb_ref[i], logits)
        mx = jnp.maximum(la, lb)
        ea = jnp.exp(la - mx)
        eb = jnp.exp(lb - mx)
        wa = pg * (ea / (ea + eb))
        wb = pg * (eb / (ea + eb))

        def ffn(w1_ref, w3_ref, w2_ref):
            a = jnp.dot(x, w1_ref[...], preferred_element_type=F32)
            b = jnp.dot(x, w3_ref[...], preferred_element_type=F32)
            hid = (a * jax.nn.sigmoid(a)) * b
            return jnp.dot(hid.astype(BF16), w2_ref[...], preferred_element_type=F32)

        y = wa * ffn(w1a_ref, w3a_ref, w2a_ref) + wb * ffn(w1b_ref, w3b_ref, w2b_ref)
        for j in range(nch):
            ys_ref[pl.ds(j, tm, stride=nch), :] = y[:, j * LANES:(j + 1) * LANES]


def _ffn(tiles, hs, rw, rb, w1, w3, w2, tm, nch):
    ta, tb, tg, tv, tblk = tiles
    n_tiles = ta.shape[0]
    d, de = w1.shape[1:]
    rows = pl.BlockSpec((tm * nch, LANES), lambda i, ta, tb, tg, tv, tblk: (tblk[i], 0))
    const = lambda shape: pl.BlockSpec(shape, lambda i, *_: (0, 0))
    wa = lambda shape: pl.BlockSpec((None,) + shape, lambda i, ta, tb, tg, tv, tblk: (ta[i], 0, 0))
    wb = lambda shape: pl.BlockSpec((None,) + shape, lambda i, ta, tb, tg, tv, tblk: (tb[i], 0, 0))
    grid_spec = pltpu.PrefetchScalarGridSpec(
        num_scalar_prefetch=5, grid=(n_tiles,),
        in_specs=[rows, const((d, LANES)), const((1, LANES)),
                  wa((d, de)), wa((d, de)), wa((de, d)), wb((d, de)), wb((d, de)), wb((de, d))],
        out_specs=pl.BlockSpec((tm * nch, LANES), lambda i, *_: (i, 0)))
    return pl.pallas_call(
        functools.partial(_ffn_kernel, tm=tm, nch=nch),
        grid_spec=grid_spec,
        out_shape=jax.ShapeDtypeStruct(hs.shape, F32),
        compiler_params=_params(("arbitrary",)),
        name="moe_ffn",
    )(ta, tb, tg, tv, tblk, hs, rw, rb, w1, w3, w2, w1, w3, w2)


def _combine_kernel(pos_ref, ys_ref, x_ref, gt_ref, fg_ref, o_ref, buf_ref, sem, *, tg, nch, final):
    i = pl.program_id(0)
    n = pl.num_programs(0)

    def rows(ref, r, k=1):
        return ref.at[pl.ds(pl.multiple_of(r * nch, nch), k * nch)]

    def fetch(tile, slot):
        def start(r):
            pltpu.make_async_copy(rows(ys_ref, pos_ref[tile * tg + r]), rows(buf_ref, slot * tg + r),
                                  sem.at[slot]).start()
        _row_dma_loop(tg, start)

    @pl.when(i == 0)
    def _():
        fetch(0, 0)

    @pl.when(i + 1 < n)
    def _():
        fetch(i + 1, (i + 1) % 2)

    slot = i % 2
    pltpu.make_async_copy(rows(ys_ref, 0, tg), rows(buf_ref, slot * tg, tg), sem.at[slot]).wait()
    base = slot * (tg * nch)
    y = jnp.concatenate([buf_ref[pl.ds(base + j, tg, stride=nch), :] for j in range(nch)], axis=-1)
    xn = x_ref[...] + gt_ref[...] * y
    if final:
        xn = (xn * lax.rsqrt(jnp.mean(xn * xn, axis=-1, keepdims=True) + EPS)) * fg_ref[...]
    o_ref[...] = xn


def _combine(pos, ys, x2, mod, l, bsz, seq, final_g, final, nch, tg=512):
    t_all, d = x2.shape
    tile = pl.BlockSpec((tg, d), lambda t, pos: (t, 0))
    grid_spec = pltpu.PrefetchScalarGridSpec(
        num_scalar_prefetch=1, grid=(t_all // tg,),
        in_specs=[pl.BlockSpec(memory_space=pl.ANY), tile,
                  pl.BlockSpec((None, 1, d), lambda t, pos: ((l * bsz + (t * tg) // seq) * 6 + 5, 0, 0)),
                  pl.BlockSpec((1, d), lambda t, pos: (0, 0))],
        out_specs=tile,
        scratch_shapes=[pltpu.VMEM((2 * tg * nch, LANES), F32), pltpu.SemaphoreType.DMA((2,))])
    return pl.pallas_call(
        functools.partial(_combine_kernel, tg=tg, nch=nch, final=final),
        grid_spec=grid_spec,
        out_shape=jax.ShapeDtypeStruct(x2.shape, F32),
        compiler_params=_params(("arbitrary",)),
        name="moe_combine",
    )(pos, ys, x2, mod, final_g.reshape(1, d))


def _moe(x, mod, l, g, rw, rb, w1, w3, w2, final_g, final, tm=512):
    bsz, seq, d = x.shape
    t_all = bsz * seq
    nch = d // LANES
    x2 = x.reshape(t_all, d)
    hlin, meta, cnt = _route(x2, mod, l, bsz, seq, g, rw, rb)

    n_tiles = t_all // tm + N_BUCKETS
    bucket = meta[0].astype(jnp.int32)
    rank = (meta[1] * RANK_SPLIT + meta[2]).astype(jnp.int32)
    count = cnt[0, :N_BUCKETS].astype(jnp.int32)
    padded = (count + tm - 1) // tm * tm
    ends = jnp.cumsum(padded)
    starts = ends - padded
    buckets = jnp.arange(N_BUCKETS, dtype=jnp.int32)
    pos = rank + jnp.sum(jnp.where(bucket[:, None] == buckets, starts, 0), axis=-1)
    n_valid = ends[-1] // tm
    tile_id = jnp.arange(n_tiles, dtype=jnp.int32)
    tblk = jnp.minimum(tile_id, n_valid - 1)
    tbucket = jnp.sum((ends <= (tblk * tm)[:, None]).astype(jnp.int32), axis=-1)
    pair = tbucket % N_PAIRS
    tgrp = tbucket // N_PAIRS
    pick = lambda table: jnp.sum(jnp.where(pair[:, None] == jnp.arange(N_PAIRS), jnp.asarray(table, jnp.int32), 0), -1)
    ta = tgrp * EXPERTS_PER_GROUP + pick(PAIR_LO)
    tb = tgrp * EXPERTS_PER_GROUP + pick(PAIR_HI)
    tv = (tile_id < n_valid).astype(jnp.int32)
    last = jnp.where(padded > 0, ends // tm - 1, -1)
    tail = jnp.where(n_valid + buckets < n_tiles, n_valid + buckets, -1)
    ztile = jnp.concatenate([last, tail]).astype(jnp.int32)

    hs = _dispatch(pos, ztile, hlin, n_tiles * tm, nch, tm)
    ys = _ffn((ta, tb, tgrp, tv, tblk), hs, rw, rb, w1, w3, w2, tm, nch)
    out = _combine(pos, ys, x2, mod, l, bsz, seq, final_g, final, nch)
    return out.reshape(bsz, seq, d)


def kernel(x, c, mod_w, mod_b, norm_mix_g, norm_ffn_g, conv_in_w, conv_w, conv_b, conv_out_w, kv_mod_w, kv_mod_b, kv_norm_g, kv_w, q_w, lam_q1, lam_k1, lam_q2, lam_k2, subln_g, o_w, router_group_w, router_group_b, router_exp_w, router_exp_b, exp_w1, exp_w3, exp_w2, final_norm_g):
    bsz, seq, d = x.shape
    depth = mod_w.shape[0]
    n_a = conv_in_w.shape[0]
    hd = d // (2 * N_HEADS)

    mod = _modulation(c, mod_w, mod_b).reshape(depth * bsz * 6, 1, d)
    kvm = _modulation(c, kv_mod_w[None], kv_mod_b[None]).reshape(bsz, 2, 1, d)
    kv_sh, kv_sc = kvm[:, 0], kvm[:, 1]

    pad = LANES - N_GROUPS - N_EXPERTS
    rw = jnp.pad(jnp.concatenate([router_group_w, router_exp_w], axis=-1), ((0, 0), (0, 0), (0, pad))).astype(BF16)
    rb = jnp.pad(jnp.concatenate([router_group_b, router_exp_b], axis=-1), ((0, 0), (0, pad)))[:, None, :]

    k = v = None
    for l in range(depth):
        if l == n_a:
            k, v = _proj(x, kv_norm_g, kv_sh, kv_sc, kv_w.astype(BF16), 1.0)
        if l < n_a:
            x = _conv_layer(x, mod, l, norm_mix_g[l], conv_in_w[l].astype(BF16), conv_w[l], conv_b[l],
                            conv_out_w[l].astype(BF16))
        else:
            j = l - n_a
            lambda_init = 0.8 - 0.6 * float(np.exp(-0.3 * l))
            m4 = mod.reshape(depth, bsz, 6, 1, d)
            (q,) = _proj(x, norm_mix_g[l], m4[l, :, 0], m4[l, :, 1], q_w[j].astype(BF16), hd ** -0.5)
            a = _diff_attention(q, k, v, lam_q1[j], lam_k1[j], lam_q2[j], lam_k2[j], subln_g[j], lambda_init)
            x = _oproj(a, o_w[j].astype(BF16), x, mod, l)
        x = _moe(x, mod, l, norm_ffn_g[l], rw[l], rb[l], exp_w1[l].astype(BF16), exp_w3[l].astype(BF16),
                 exp_w2[l].astype(BF16), final_norm_g, final=(l == depth - 1))
    return x
```

```python
import functools

import jax
import jax.numpy as jnp
import numpy as np
from jax import lax
from jax.experimental import pallas as pl
from jax.experimental.pallas import tpu as pltpu

N_HEADS = 8
N_GROUPS = 4
EXPERTS_PER_GROUP = 4
N_EXPERTS = N_GROUPS * EXPERTS_PER_GROUP
CONV_WIDTH = 3
EPS = 1e-6
NEG_INF = -1e30
LANES = 128
SUBLANES = 8
POS_SPLIT = 128
VMEM_LIMIT = 48 * 1024 * 1024
FFN_VMEM_LIMIT = 56 * 1024 * 1024

F32 = jnp.float32
BF16 = jnp.bfloat16


def _params(sem):
    return pltpu.CompilerParams(dimension_semantics=sem, vmem_limit_bytes=VMEM_LIMIT)


def _norm_mod(x, g, sh, sc):
    ms = jnp.mean(x * x, axis=-1, keepdims=True)
    y = (x * lax.rsqrt(ms + EPS)) * g
    return y * (1.0 + sc) + sh


PAIR_LO = (0, 0, 0, 1, 1, 2)
PAIR_HI = (1, 2, 3, 2, 3, 3)
N_PAIRS = len(PAIR_LO)
N_BUCKETS = N_GROUPS * N_PAIRS
DMA_UNROLL = 8
RANK_SPLIT = 256


def _route_tile(x, g_ref, sh_ref, sc_ref, rw_ref, rb_ref, tri_ref, hlin_ref, meta_ref, cnt_ref, run_ref):
    ts, d = x.shape
    nch = d // LANES

    @pl.when((pl.program_id(0) == 0) & (pl.program_id(1) == 0))
    def _():
        run_ref[...] = jnp.zeros_like(run_ref)

    h = _norm_mod(x, g_ref[...], sh_ref[...], sc_ref[...]).astype(BF16)
    for j in range(nch):
        hlin_ref[pl.ds(j, ts, stride=nch), :] = h[:, j * LANES:(j + 1) * LANES].astype(F32)
    lt = lax.dot_general(rw_ref[...], h, (((1,), (1,)), ((), ())), preferred_element_type=F32) + rb_ref[...]
    row = lambda r: lt[r:r + 1, :]

    def first_argmax(vals):
        best, idx = vals[0], jnp.zeros_like(vals[0])
        for k in range(1, len(vals)):
            better = vals[k] > best
            idx = jnp.where(better, float(k), idx)
            best = jnp.where(better, vals[k], best)
        return idx

    g_sel = first_argmax([row(g) for g in range(N_GROUPS)])
    e = []
    for k in range(EXPERTS_PER_GROUP):
        ek = row(N_GROUPS + k)
        for g in range(1, N_GROUPS):
            ek = jnp.where(g_sel == float(g), row(N_GROUPS + EXPERTS_PER_GROUP * g + k), ek)
        e.append(ek)
    i1 = first_argmax(e)
    i2 = first_argmax([jnp.where(i1 == float(k), NEG_INF, e[k]) for k in range(EXPERTS_PER_GROUP)])
    a = jnp.minimum(i1, i2)
    b = jnp.maximum(i1, i2)
    bucket = N_PAIRS * g_sel + (a * (7.0 - a) * 0.5 + b - a - 1.0)
    sub = lax.broadcasted_iota(jnp.int32, lt.shape, 0).astype(F32)
    onehot = jnp.where(sub == bucket, 1.0, 0.0)
    before = jnp.dot(onehot.astype(BF16), tri_ref[...], preferred_element_type=F32) + run_ref[...]
    rank = jnp.sum(onehot * before, axis=0, keepdims=True)
    rank_hi = jnp.floor(rank * (1.0 / RANK_SPLIT))
    sub8 = lax.broadcasted_iota(jnp.int32, meta_ref.shape, 0)
    meta_ref[...] = jnp.where(sub8 == 0, bucket, jnp.where(sub8 == 1, rank_hi,
                                                           jnp.where(sub8 == 2, rank - RANK_SPLIT * rank_hi, 0.0)))
    run = run_ref[...] + jnp.sum(onehot, axis=1, keepdims=True)
    run_ref[...] = run
    cnt_ref[...] = jnp.broadcast_to(run, cnt_ref.shape)


def _route_io(mod, l, bsz, seq, d, ts, g, rw, rb):
    assert bsz * seq <= RANK_SPLIT * RANK_SPLIT
    nch = d // LANES
    nt = seq // ts
    row = lambda i: pl.BlockSpec((None, 1, d), lambda b, s, i=i: ((l * bsz + b) * 6 + i, 0, 0))
    const = lambda shape: pl.BlockSpec(shape, lambda b, s: (0, 0))
    in_specs = [const((1, d)), row(3), row(4), const((LANES, d)), const((LANES, 1)), const((ts, ts))]
    inputs = [g.reshape(1, d), mod, mod, rw.T, rb.reshape(LANES, 1), jnp.triu(jnp.ones((ts, ts), BF16), 1)]
    out_specs = [pl.BlockSpec((ts * nch, LANES), lambda b, s: (b * nt + s, 0)),
                 pl.BlockSpec((SUBLANES, ts), lambda b, s: (0, b * nt + s)), const((LANES, LANES))]
    out_shape = [jax.ShapeDtypeStruct((bsz * seq * nch, LANES), F32), jax.ShapeDtypeStruct((SUBLANES, bsz * seq), F32),
                 jax.ShapeDtypeStruct((LANES, LANES), F32)]
    return in_specs, inputs, out_specs, out_shape, pltpu.VMEM((LANES, 1), F32)


def _mod_kernel(c_ref, w_ref, b_ref, o_ref):
    c = c_ref[...]
    ca = c * jax.nn.sigmoid(c)
    o_ref[...] = jnp.dot(ca, w_ref[...], preferred_element_type=F32,
                         precision=lax.Precision.HIGHEST) + b_ref[...]


def _modulation(c, w, b, bn=1024):
    nl, d, n = w.shape
    bsz = c.shape[0]
    return pl.pallas_call(
        _mod_kernel,
        grid=(nl, n // bn),
        in_specs=[pl.BlockSpec((bsz, d), lambda l, j: (0, 0)),
                  pl.BlockSpec((None, d, bn), lambda l, j: (l, 0, j)),
                  pl.BlockSpec((None, 1, bn), lambda l, j: (l, 0, j))],
        out_specs=pl.BlockSpec((None, bsz, bn), lambda l, j: (l, 0, j)),
        out_shape=jax.ShapeDtypeStruct((nl, bsz, n), F32),
        compiler_params=_params(("arbitrary", "arbitrary")),
        name="modulation",
    )(c, w, b.reshape(nl, 1, n))


def _conv_kernel(x_ref, g_ref, sh_ref, sc_ref, gt_ref, win_ref, cw_ref, cb_ref, wout_ref, *rest, d, ts):
    route_in, (o_ref, hlin_ref, meta_ref, cnt_ref, carry_ref, run_ref) = rest[:6], rest[6:]

    @pl.when(pl.program_id(1) == 0)
    def _():
        carry_ref[...] = jnp.zeros_like(carry_ref)

    x = x_ref[...]
    h = _norm_mod(x, g_ref[...], sh_ref[...], sc_ref[...]).astype(BF16)
    c_gate = jnp.dot(h, win_ref[:, d:2 * d], preferred_element_type=F32)
    v = jnp.dot(h, win_ref[:, 2 * d:3 * d], preferred_element_type=F32)
    z = c_gate * v
    prev = carry_ref[...]
    row = lax.broadcasted_iota(jnp.int32, (ts, 1), 0)
    z1 = jnp.where(row == 0, prev[SUBLANES - 1:SUBLANES], pltpu.roll(z, 1, 0))
    z2 = jnp.where(row == 0, prev[SUBLANES - 2:SUBLANES - 1],
                   jnp.where(row == 1, prev[SUBLANES - 1:SUBLANES], pltpu.roll(z, 2, 0)))
    carry_ref[...] = z[ts - SUBLANES:ts]
    cw = cw_ref[...]
    zc = cw[0:1] * z2 + cw[1:2] * z1 + cw[2:3] * z + cb_ref[...]
    b_gate = jnp.dot(h, win_ref[:, 0:d], preferred_element_type=F32)
    y = jnp.dot((b_gate * zc).astype(BF16), wout_ref[...], preferred_element_type=F32)
    xn = x + gt_ref[...] * y
    o_ref[...] = xn
    _route_tile(xn, *route_in, hlin_ref, meta_ref, cnt_ref, run_ref)


def _conv_layer(x, mod, l, g, w_in, cw, cb, w_out, route_args, ts=512):
    bsz, seq, d = x.shape
    row = lambda i: pl.BlockSpec((None, 1, d), lambda b, s, i=i: ((l * bsz + b) * 6 + i, 0, 0))
    const2 = lambda shape: pl.BlockSpec(shape, lambda b, s: (0, 0))
    xspec = pl.BlockSpec((None, ts, d), lambda b, s: (b, s, 0))
    r_in, r_args, r_out, r_shape, r_scratch = _route_io(mod, l, bsz, seq, d, ts, *route_args)
    return pl.pallas_call(
        functools.partial(_conv_kernel, d=d, ts=ts),
        grid=(bsz, seq // ts),
        in_specs=[xspec, const2((1, d)), row(0), row(1), row(2), const2((d, 3 * d)),
                  const2((SUBLANES, d)), const2((1, d)), const2((d, d))] + r_in,
        out_specs=[xspec] + r_out,
        out_shape=[jax.ShapeDtypeStruct(x.shape, F32)] + r_shape,
        scratch_shapes=[pltpu.VMEM((SUBLANES, d), F32), r_scratch],
        compiler_params=_params(("arbitrary", "arbitrary")),
        name="conv_layer",
    )(x, g.reshape(1, d), mod, mod, mod, w_in, jnp.pad(cw, ((0, SUBLANES - CONV_WIDTH), (0, 0))),
      cb.reshape(1, d), w_out, *r_args)


def _proj_kernel(x_ref, g_ref, sh_ref, sc_ref, w_ref, *o_refs, d, scale):
    h = _norm_mod(x_ref[...], g_ref[...], sh_ref[...], sc_ref[...]).astype(BF16)
    for j, o_ref in enumerate(o_refs):
        y = jnp.dot(h, w_ref[:, j * d:(j + 1) * d], preferred_element_type=F32)
        o_ref[...] = (y * scale).astype(o_ref.dtype)


def _proj(x, g, sh, sc, w, scale, ts=512):
    bsz, seq, d = x.shape
    n_out = w.shape[1] // d
    xspec = pl.BlockSpec((None, ts, d), lambda b, s: (b, s, 0))
    row = pl.BlockSpec((None, 1, d), lambda b, s: (b, 0, 0))
    return pl.pallas_call(
        functools.partial(_proj_kernel, d=d, scale=scale),
        grid=(bsz, seq // ts),
        in_specs=[xspec, pl.BlockSpec((1, d), lambda b, s: (0, 0)), row, row,
                  pl.BlockSpec(w.shape, lambda b, s: (0, 0))],
        out_specs=[xspec] * n_out,
        out_shape=[jax.ShapeDtypeStruct(x.shape, BF16)] * n_out,
        compiler_params=_params(("arbitrary", "arbitrary")),
        name="proj",
    )(x, g.reshape(1, d), sh, sc, w)


def _flash_kernel(slope_ref, q_ref, k_ref, v_ref, lq1_ref, lk1_ref, lq2_ref, lk2_ref, g_ref,
                  o_ref, qq_ref, kk_ref, vv_ref, s0_ref, s1_ref, m_ref, acc_ref, *, tq, tk, rc, hd, lambda_init):
    h = pl.program_id(1)
    qi = pl.program_id(2)
    seq, vd = k_ref.shape
    q0 = qi * tq

    @pl.when(qi == 0)
    def _():
        pos = lax.broadcasted_iota(jnp.int32, (seq, vd), 0)
        col = lax.broadcasted_iota(jnp.int32, (seq, vd), 1)
        lo = pos % POS_SPLIT
        kk_ref[:, 0:vd] = k_ref[...]
        kk_ref[:, vd:2 * vd] = jnp.where(col == 0, pos - lo, jnp.where(col == 1, lo, 0)).astype(F32).astype(BF16)
        vv_ref[:, 0:vd] = v_ref[...]
        vv_ref[:, vd:2 * vd] = jnp.ones((seq, vd), BF16)

    q = q_ref[...]
    lane = lax.broadcasted_iota(jnp.int32, q.shape, 1)
    zero = jnp.zeros_like(q)
    slope_cols = jnp.where(lane < 2, slope_ref[h], 0.0).astype(BF16)
    qq_ref[0:tq, 0:vd] = jnp.where(lane < hd, q, zero)
    qq_ref[tq:2 * tq, 0:vd] = jnp.where(lane >= hd, q, zero)
    qq_ref[0:tq, vd:2 * vd] = slope_cols
    qq_ref[tq:2 * tq, vd:2 * vd] = slope_cols
    m_ref[...] = jnp.full_like(m_ref, NEG_INF)
    acc_ref[...] = jnp.zeros_like(acc_ref)

    def scores(j, s_ref):
        kj = kk_ref[pl.ds(pl.multiple_of(j * tk, tk), tk), :]
        s_ref[...] = lax.dot_general(qq_ref[...], kj, (((1,), (1,)), ((), ())), preferred_element_type=F32)

    def update(j, s_ref, masked):
        vj = vv_ref[pl.ds(pl.multiple_of(j * tk, tk), tk), :]
        for c in range(2 * tq // rc):
            rows = pl.ds(c * rc, rc)
            s = s_ref[rows, :]
            if masked:
                qpos = lax.broadcasted_iota(jnp.int32, (rc, tk), 0) + (c * rc) % tq
                kpos = lax.broadcasted_iota(jnp.int32, (rc, tk), 1)
                s = jnp.where(qpos >= kpos, s, NEG_INF)
            m_prev = m_ref[rows, :]
            m_new = jnp.maximum(m_prev, jnp.max(s, axis=-1, keepdims=True))
            alpha = jnp.exp(m_prev - m_new)
            p = jnp.exp(s - jnp.tile(m_new, (1, tk // LANES)))
            pv = jnp.dot(p.astype(BF16), vj, preferred_element_type=F32)
            acc_ref[rows, :] = jnp.tile(alpha, (1, 2 * vd // LANES)) * acc_ref[rows, :] + pv
            m_ref[rows, :] = m_new

    scores(0, s0_ref)

    def body(i, carry):
        j = 2 * i
        scores(j + 1, s1_ref)
        update(j, s0_ref, False)
        scores(j + 2, s0_ref)
        update(j + 1, s1_ref, False)
        return carry

    lax.fori_loop(0, qi // 2, body, 0)

    @pl.when(qi % 2 == 0)
    def _():
        update(qi, s0_ref, True)

    @pl.when(qi % 2 == 1)
    def _():
        scores(qi, s1_ref)
        update(qi - 1, s0_ref, False)
        update(qi, s1_ref, True)

    lam = (jnp.exp(jnp.sum(lq1_ref[...] * lk1_ref[...], keepdims=True))
           - jnp.exp(jnp.sum(lq2_ref[...] * lk2_ref[...], keepdims=True)) + lambda_init)
    acc = acc_ref[...]
    o12 = acc[:, 0:vd] / acc[:, vd:2 * vd]
    o = o12[0:tq] - lam * o12[tq:2 * tq]
    on =o * lax.rsqrt(jnp.mean(o * o, axis=-1, keepdims=True) + EPS) * g_ref[...]
    o_ref[...] = (on * (1.0 - lambda_init)).astype(o_ref.dtype)


def _diff_attention(q, k, v, lq1, lk1, lq2, lk2, subln_g, lambda_init, tq=512, tk=512, rc=512):
    bsz, seq, d = q.shape
    vd = d // N_HEADS
    hd = vd // 2
    assert vd == LANES and seq <= POS_SPLIT * 256 and 8 % N_HEADS == 0 and tq == tk
    slopes = jnp.exp2(-8.0 * (jnp.arange(N_HEADS, dtype=F32) + 1.0) / N_HEADS)
    vec = lambda n: pl.BlockSpec((1, n), lambda b, h, i, sl: (0, 0))
    kvspec = pl.BlockSpec((None, seq, vd), lambda b, h, i, sl: (b, 0, h))
    qspec = pl.BlockSpec((None, tq, vd), lambda b, h, i, sl: (b, i, h))
    grid_spec = pltpu.PrefetchScalarGridSpec(
        num_scalar_prefetch=1,
        grid=(bsz, N_HEADS, seq // tq),
        in_specs=[qspec, kvspec, kvspec, vec(hd), vec(hd), vec(hd), vec(hd), vec(vd)],
        out_specs=qspec,
        scratch_shapes=[pltpu.VMEM((2 * tq, 2 * vd), BF16), pltpu.VMEM((seq, 2 * vd), BF16),
                        pltpu.VMEM((seq, 2 * vd), BF16), pltpu.VMEM((2 * tq, tk), F32), pltpu.VMEM((2 * tq, tk), F32),
                        pltpu.VMEM((2 * tq, LANES), F32),
                        pltpu.VMEM((2 * tq, 2 * vd), F32)],
    )
    return pl.pallas_call(
        functools.partial(_flash_kernel, tq=tq, tk=tk, rc=rc, hd=hd, lambda_init=lambda_init),
        grid_spec=grid_spec,
        out_shape=jax.ShapeDtypeStruct(q.shape, BF16),
        compiler_params=_params(("arbitrary", "arbitrary", "arbitrary")),
        name="diff_attention",
    )(slopes, q, k, v, lq1.reshape(1, hd), lk1.reshape(1, hd), lq2.reshape(1, hd), lk2.reshape(1, hd),
      subln_g.reshape(1, vd))


def _oproj_kernel(a_ref, w_ref, x_ref, gt_ref, *rest):
    route_in, (o_ref, hlin_ref, meta_ref, cnt_ref, run_ref) = rest[:6], rest[6:]
    y = jnp.dot(a_ref[...], w_ref[...], preferred_element_type=F32)
    xn = x_ref[...] + gt_ref[...] * y
    o_ref[...] = xn
    _route_tile(xn, *route_in, hlin_ref, meta_ref, cnt_ref, run_ref)


def _oproj(a, w, x, mod, l, route_args, ts=512):
    bsz, seq, d = x.shape
    xspec = pl.BlockSpec((None, ts, d), lambda b, s: (b, s, 0))
    r_in, r_args, r_out, r_shape, r_scratch = _route_io(mod, l, bsz, seq, d, ts, *route_args)
    return pl.pallas_call(
        _oproj_kernel,
        grid=(bsz, seq // ts),
        in_specs=[xspec, pl.BlockSpec((d, d), lambda b, s: (0, 0)), xspec,
                  pl.BlockSpec((None, 1, d), lambda b, s: ((l * bsz + b) * 6 + 2, 0, 0))] + r_in,
        out_specs=[xspec] + r_out,
        out_shape=[jax.ShapeDtypeStruct(x.shape, F32)] + r_shape,
        scratch_shapes=[r_scratch],
        compiler_params=_params(("arbitrary", "arbitrary")),
        name="attn_out",
    )(a, w, x, mod, *r_args)


def _row_dma_loop(n_rows, start_fn):
    def body(i, carry):
        for u in range(DMA_UNROLL):
            start_fn(i * DMA_UNROLL + u)
        return carry
    lax.fori_loop(0, n_rows // DMA_UNROLL, body, 0)


def _dispatch_kernel(pos_ref, ztile_ref, hlin_ref, hs_ref, zero_ref, sem, zsem, *, nch, chunk, tm):
    def rows(ref, r, n=1):
        return ref.at[pl.ds(pl.multiple_of(r * nch, nch), n * nch)]

    c = pl.program_id(0)

    @pl.when(c == 0)
    def _():
        zero_ref[...] = jnp.zeros_like(zero_ref)
        fill = lambda k: pltpu.make_async_copy(zero_ref, rows(hs_ref, ztile_ref[k] * tm, tm), zsem)
        for k in range(ztile_ref.shape[0]):
            @pl.when(ztile_ref[k] >= 0)
            def _(k=k):
                fill(k).start()
        for k in range(ztile_ref.shape[0]):
            @pl.when(ztile_ref[k] >= 0)
            def _(k=k):
                fill(k).wait()

    _row_dma_loop(chunk, lambda r: pltpu.make_async_copy(
        rows(hlin_ref, r), rows(hs_ref, pos_ref[c * chunk + r]), sem).start())
    pltpu.make_async_copy(hlin_ref, rows(hs_ref, 0, chunk), sem).wait()


def _dispatch(pos, ztile, hlin, n_rows, nch, tm, chunk=1024):
    t_all = pos.shape[0]
    grid_spec = pltpu.PrefetchScalarGridSpec(
        num_scalar_prefetch=2, grid=(t_all // chunk,),
        in_specs=[pl.BlockSpec((chunk * nch, LANES), lambda c, pos, zt: (c, 0))],
        out_specs=pl.BlockSpec(memory_space=pl.ANY),
        scratch_shapes=[pltpu.VMEM((tm * nch, LANES), F32), pltpu.SemaphoreType.DMA(()),
                        pltpu.SemaphoreType.DMA(())])
    return pl.pallas_call(
        functools.partial(_dispatch_kernel, nch=nch, chunk=chunk, tm=tm),
        grid_spec=grid_spec,
        out_shape=jax.ShapeDtypeStruct((n_rows * nch, LANES), F32),
        compiler_params=_params(("arbitrary",)),
        name="moe_dispatch",
    )(pos, ztile, hlin)


def _ffn_kernel(ta_ref, tb_ref, tg_ref, tv_ref, tblk_ref, tnew_ref, hs_ref, rw_ref, rb_ref, *rest, tm, nch):
    del tblk_ref
    w_f32, ys_ref, w_bf16 = rest[:6], rest[6], rest[7:]
    w1a_ref, w3a_ref, w2a_ref, w1b_ref, w3b_ref, w2b_ref = w_bf16
    i = pl.program_id(0)

    @pl.when(tv_ref[i] == 0)
    def _():
        ys_ref[...] = jnp.zeros_like(ys_ref)

    @pl.when(tnew_ref[i] == 1)
    def _():
        for src, dst in zip(w_f32, w_bf16):
            dst[...] = src[...].astype(BF16)

    @pl.when(tv_ref[i] == 1)
    def _():
        x = jnp.concatenate([hs_ref[pl.ds(j, tm, stride=nch), :] for j in range(nch)], axis=-1).astype(BF16)
        logits = jnp.dot(x, rw_ref[...], preferred_element_type=F32) + rb_ref[...]
        lane = lax.broadcasted_iota(jnp.int32, logits.shape, 1)
        pick = lambda col, val: jnp.sum(jnp.where(lane == col, val, 0.0), axis=-1, keepdims=True)
        gl = jnp.where(lane < N_GROUPS, logits, NEG_INF)
        eg = jnp.exp(gl - jnp.max(gl, axis=-1, keepdims=True))
        pg = pick(tg_ref[i], eg) / jnp.sum(eg, axis=-1, keepdims=True)
        la = pick(N_GROUPS + ta_ref[i], logits)
        lb = pick(N_GROUPS + tb_ref[i], logits)
        mx = jnp.maximum(la, lb)
        ea = jnp.exp(la - mx)
        eb = jnp.exp(lb - mx)
        wa = pg * (ea / (ea + eb))
        wb = pg * (eb / (ea + eb))

        def ffn(w1_ref, w3_ref, w2_ref):
            a = jnp.dot(x, w1_ref[...], preferred_element_type=F32)
            b = jnp.dot(x, w3_ref[...], preferred_element_type=F32)
            hid = (a * jax.nn.sigmoid(a)) * b
            return jnp.dot(hid.astype(BF16), w2_ref[...], preferred_element_type=F32)

        y = wa * ffn(w1a_ref, w3a_ref, w2a_ref) + wb * ffn(w1b_ref, w3b_ref, w2b_ref)
        for j in range(nch):
            ys_ref[pl.ds(j, tm, stride=nch), :] = y[:, j * LANES:(j + 1) * LANES]


def _ffn(tiles, hs, rw, rb, w1, w3, w2, tm, nch):
    ta = tiles[0]
    n_tiles = ta.shape[0]
    d, de = w1.shape[1:]
    rows = pl.BlockSpec((tm * nch, LANES), lambda i, ta, tb, tg, tv, tblk, tnew: (tblk[i], 0))
    const = lambda shape: pl.BlockSpec(shape, lambda i, *_: (0, 0))
    wa = lambda shape: pl.BlockSpec((None,) + shape, lambda i, ta, tb, tg, tv, tblk, tnew: (ta[i], 0, 0))
    wb = lambda shape: pl.BlockSpec((None,) + shape, lambda i, ta, tb, tg, tv, tblk, tnew: (tb[i], 0, 0))
    grid_spec = pltpu.PrefetchScalarGridSpec(
        num_scalar_prefetch=6, grid=(n_tiles,),
        in_specs=[rows, const((d, LANES)), const((1, LANES)),
                  wa((d, de)), wa((d, de)), wa((de, d)), wb((d, de)), wb((d, de)), wb((de, d))],
        out_specs=pl.BlockSpec((tm * nch, LANES), lambda i, *_: (i, 0)),
        scratch_shapes=[pltpu.VMEM(s, BF16) for s in ((d, de), (d, de), (de, d)) * 2])
    return pl.pallas_call(
        functools.partial(_ffn_kernel, tm=tm, nch=nch),
        grid_spec=grid_spec,
        out_shape=jax.ShapeDtypeStruct(hs.shape, F32),
        compiler_params=pltpu.CompilerParams(dimension_semantics=("arbitrary",), vmem_limit_bytes=FFN_VMEM_LIMIT),
        name="moe_ffn",
    )(*tiles, hs, rw, rb, w1, w3, w2, w1, w3, w2)


def _combine_kernel(pos_ref, ys_ref, x_ref, gt_ref, fg_ref, o_ref, buf_ref, sem, *, tg, nch, final):
    i = pl.program_id(0)
    n = pl.num_programs(0)

    def rows(ref, r, k=1):
        return ref.at[pl.ds(pl.multiple_of(r * nch, nch), k * nch)]

    def fetch(tile, slot):
        def start(r):
            pltpu.make_async_copy(rows(ys_ref, pos_ref[tile * tg + r]), rows(buf_ref, slot * tg + r),
                                  sem.at[slot]).start()
        _row_dma_loop(tg, start)

    @pl.when(i == 0)
    def _():
        fetch(0, 0)

    @pl.when(i + 1 < n)
    def _():
        fetch(i + 1, (i + 1) % 2)

    slot = i % 2
    pltpu.make_async_copy(rows(ys_ref, 0, tg), rows(buf_ref, slot * tg, tg), sem.at[slot]).wait()
    base = slot * (tg * nch)
    y = jnp.concatenate([buf_ref[pl.ds(base + j, tg, stride=nch), :] for j in range(nch)], axis=-1)
    xn = x_ref[...] + gt_ref[...] * y
    if final:
        xn = (xn * lax.rsqrt(jnp.mean(xn * xn, axis=-1, keepdims=True) + EPS)) * fg_ref[...]
    o_ref[...] = xn


def _combine(pos, ys, x2, mod, l, bsz, seq, final_g, final, nch, tg=512):
    t_all, d = x2.shape
    tile = pl.BlockSpec((tg, d), lambda t, pos: (t, 0))
    grid_spec = pltpu.PrefetchScalarGridSpec(
        num_scalar_prefetch=1, grid=(t_all // tg,),
        in_specs=[pl.BlockSpec(memory_space=pl.ANY), tile,
                  pl.BlockSpec((None, 1, d), lambda t, pos: ((l * bsz + (t * tg) // seq) * 6 + 5, 0, 0)),
                  pl.BlockSpec((1, d), lambda t, pos: (0, 0))],
        out_specs=tile,
        scratch_shapes=[pltpu.VMEM((2 * tg * nch, LANES), F32), pltpu.SemaphoreType.DMA((2,))])
    return pl.pallas_call(
        functools.partial(_combine_kernel, tg=tg, nch=nch, final=final),
        grid_spec=grid_spec,
        out_shape=jax.ShapeDtypeStruct(x2.shape, F32),
        compiler_params=_params(("arbitrary",)),
        name="moe_combine",
    )(pos, ys, x2, mod, final_g.reshape(1, d))


def _moe(x, routed, mod, l, rw, rb, w1, w3, w2, final_g, final, tm=512):
    bsz, seq, d = x.shape
    t_all = bsz * seq
    nch = d // LANES
    x2 = x.reshape(t_all, d)
    hlin, meta, cnt = routed

    n_tiles = t_all // tm + N_BUCKETS
    bucket = meta[0].astype(jnp.int32)
    rank = (meta[1] * RANK_SPLIT + meta[2]).astype(jnp.int32)
    count = cnt[:N_BUCKETS, 0].astype(jnp.int32)
    padded = (count + tm - 1) // tm * tm
    ends = jnp.cumsum(padded)
    starts = ends - padded
    buckets = jnp.arange(N_BUCKETS, dtype=jnp.int32)
    pos = rank + jnp.sum(jnp.where(bucket[:, None] == buckets, starts, 0), axis=-1)
    n_valid = ends[-1] // tm
    tile_id = jnp.arange(n_tiles, dtype=jnp.int32)
    tblk = jnp.minimum(tile_id, n_valid - 1)
    tbucket = jnp.sum((ends <= (tblk * tm)[:, None]).astype(jnp.int32), axis=-1)
    pair = tbucket % N_PAIRS
    tgrp = tbucket // N_PAIRS
    pick = lambda table: jnp.sum(jnp.where(pair[:, None] == jnp.arange(N_PAIRS), jnp.asarray(table, jnp.int32), 0), -1)
    ta = tgrp * EXPERTS_PER_GROUP + pick(PAIR_LO)
    tb = tgrp * EXPERTS_PER_GROUP + pick(PAIR_HI)
    tv = (tile_id < n_valid).astype(jnp.int32)
    tnew = jnp.concatenate([jnp.ones((1,), jnp.int32), (tbucket[1:] != tbucket[:-1]).astype(jnp.int32)])
    last = jnp.where(padded > 0, ends // tm - 1, -1)
    tail = jnp.where(n_valid + buckets < n_tiles, n_valid + buckets, -1)
    ztile = jnp.concatenate([last, tail]).astype(jnp.int32)

    hs = _dispatch(pos, ztile, hlin, n_tiles * tm, nch, tm)
    ys = _ffn((ta, tb, tgrp, tv, tblk, tnew), hs, rw, rb, w1, w3, w2, tm, nch)
    out = _combine(pos, ys, x2, mod, l, bsz, seq, final_g, final, nch)
    return out.reshape(bsz, seq, d)


def kernel(x, c, mod_w, mod_b, norm_mix_g, norm_ffn_g, conv_in_w, conv_w, conv_b, conv_out_w, kv_mod_w, kv_mod_b, kv_norm_g, kv_w, q_w, lam_q1, lam_k1, lam_q2, lam_k2, subln_g, o_w, router_group_w, router_group_b, router_exp_w, router_exp_b, exp_w1, exp_w3, exp_w2, final_norm_g):
    bsz, seq, d = x.shape
    depth = mod_w.shape[0]
    n_a = conv_in_w.shape[0]
    hd = d // (2 * N_HEADS)

    mod = _modulation(c, mod_w, mod_b).reshape(depth * bsz * 6, 1, d)
    kvm = _modulation(c, kv_mod_w[None], kv_mod_b[None]).reshape(bsz, 2, 1, d)
    kv_sh, kv_sc = kvm[:, 0], kvm[:, 1]

    pad = LANES - N_GROUPS - N_EXPERTS
    rw = jnp.pad(jnp.concatenate([router_group_w, router_exp_w], axis=-1), ((0, 0), (0, 0), (0, pad))).astype(BF16)
    rb = jnp.pad(jnp.concatenate([router_group_b, router_exp_b], axis=-1), ((0, 0), (0, pad)))[:, None, :]

    k = v = None
    for l in range(depth):
        if l == n_a:
            k, v = _proj(x, kv_norm_g, kv_sh, kv_sc, kv_w.astype(BF16), 1.0)
        route_args = (norm_ffn_g[l], rw[l], rb[l])
        if l < n_a:
            x, *routed = _conv_layer(x, mod, l, norm_mix_g[l], conv_in_w[l].astype(BF16), conv_w[l], conv_b[l],
                                     conv_out_w[l].astype(BF16), route_args)
        else:
            j = l - n_a
            lambda_init = 0.8 - 0.6 * float(np.exp(-0.3 * l))
            m4 = mod.reshape(depth, bsz, 6, 1, d)
            (q,) = _proj(x, norm_mix_g[l], m4[l, :, 0], m4[l, :, 1], q_w[j].astype(BF16), hd ** -0.5)
            a = _diff_attention(q, k, v, lam_q1[j], lam_k1[j], lam_q2[j], lam_k2[j], subln_g[j], lambda_init)
            x, *routed = _oproj(a, o_w[j].astype(BF16), x, mod, l, route_args)
        x = _moe(x, routed, mod, l, rw[l], rb[l], exp_w1[l], exp_w3[l], exp_w2[l], final_norm_g,
                 final=(l == depth - 1))
    return x
```

```python
import functools

import jax
import jax.numpy as jnp
import numpy as np
from jax import lax
from jax.experimental import pallas as pl
from jax.experimental.pallas import tpu as pltpu

N_HEADS = 8
N_GROUPS = 4
EXPERTS_PER_GROUP = 4
N_EXPERTS = N_GROUPS * EXPERTS_PER_GROUP
CONV_WIDTH = 3
EPS = 1e-6
NEG_INF = -1e30
LANES = 128
SUBLANES = 8
POS_SPLIT = 128
VMEM_LIMIT = 48 * 1024 * 1024
FFN_VMEM_LIMIT = 56 * 1024 * 1024

F32 = jnp.float32
BF16 = jnp.bfloat16


def _params(sem):
    return pltpu.CompilerParams(dimension_semantics=sem, vmem_limit_bytes=VMEM_LIMIT)


def _norm_mod(x, g, sh, sc):
    ms = jnp.mean(x * x, axis=-1, keepdims=True)
    y = (x * lax.rsqrt(ms + EPS)) * g
    return y * (1.0 + sc) + sh


PAIR_LO = (0, 0, 0, 1, 1, 2)
PAIR_HI = (1, 2, 3, 2, 3, 3)
N_PAIRS = len(PAIR_LO)
N_BUCKETS = N_GROUPS * N_PAIRS
DMA_UNROLL = 8
RANK_SPLIT = 256


def _route_tile(x, g_ref, sh_ref, sc_ref, rw_ref, rb_ref, tri_ref, hlin_ref, meta_ref, cnt_ref, run_ref):
    ts, d = x.shape
    nch = d // LANES

    @pl.when((pl.program_id(0) == 0) & (pl.program_id(1) == 0))
    def _():
        run_ref[...] = jnp.zeros_like(run_ref)

    h = _norm_mod(x, g_ref[...], sh_ref[...], sc_ref[...]).astype(BF16)
    for j in range(nch):
        hlin_ref[pl.ds(j, ts, stride=nch), :] = h[:, j * LANES:(j + 1) * LANES].astype(F32)
    lt = lax.dot_general(rw_ref[...], h, (((1,), (1,)), ((), ())), preferred_element_type=F32) + rb_ref[...]
    row = lambda r: lt[r:r + 1, :]

    def first_argmax(vals):
        best, idx = vals[0], jnp.zeros_like(vals[0])
        for k in range(1, len(vals)):
            better = vals[k] > best
            idx = jnp.where(better, float(k), idx)
            best = jnp.where(better, vals[k], best)
        return idx

    g_sel = first_argmax([row(g) for g in range(N_GROUPS)])
    e = []
    for k in range(EXPERTS_PER_GROUP):
        ek = row(N_GROUPS + k)
        for g in range(1, N_GROUPS):
            ek = jnp.where(g_sel == float(g), row(N_GROUPS + EXPERTS_PER_GROUP * g + k), ek)
        e.append(ek)
    i1 = first_argmax(e)
    i2 = first_argmax([jnp.where(i1 == float(k), NEG_INF, e[k]) for k in range(EXPERTS_PER_GROUP)])
    a = jnp.minimum(i1, i2)
    b = jnp.maximum(i1, i2)
    bucket = N_PAIRS * g_sel + (a * (7.0 - a) * 0.5 + b - a - 1.0)
    sub = lax.broadcasted_iota(jnp.int32, lt.shape, 0).astype(F32)
    onehot = jnp.where(sub == bucket, 1.0, 0.0)
    before = jnp.dot(onehot.astype(BF16), tri_ref[...], preferred_element_type=F32) + run_ref[...]
    rank = jnp.sum(onehot * before, axis=0, keepdims=True)
    rank_hi = jnp.floor(rank * (1.0 / RANK_SPLIT))
    sub8 = lax.broadcasted_iota(jnp.int32, meta_ref.shape, 0)
    meta_ref[...] = jnp.where(sub8 == 0, bucket, jnp.where(sub8 == 1, rank_hi,
                                                           jnp.where(sub8 == 2, rank - RANK_SPLIT * rank_hi, 0.0)))
    run = run_ref[...] + jnp.sum(onehot, axis=1, keepdims=True)
    run_ref[...] = run
    cnt_ref[...] = jnp.broadcast_to(run, cnt_ref.shape)


def _route_io(mod, l, bsz, seq, d, ts, g, rw, rb):
    assert bsz * seq <= RANK_SPLIT * RANK_SPLIT
    nch = d // LANES
    nt = seq // ts
    row = lambda i: pl.BlockSpec((None, 1, d), lambda b, s, i=i: ((l * bsz + b) * 6 + i, 0, 0))
    const = lambda shape: pl.BlockSpec(shape, lambda b, s: (0, 0))
    in_specs = [const((1, d)), row(3), row(4), const((LANES, d)), const((LANES, 1)), const((ts, ts))]
    inputs = [g.reshape(1, d), mod, mod, rw.T, rb.reshape(LANES, 1), jnp.triu(jnp.ones((ts, ts), BF16), 1)]
    out_specs = [pl.BlockSpec((ts * nch, LANES), lambda b, s: (b * nt + s, 0)),
                 pl.BlockSpec((SUBLANES, ts), lambda b, s: (0, b * nt + s)), const((LANES, LANES))]
    out_shape = [jax.ShapeDtypeStruct((bsz * seq * nch, LANES), F32), jax.ShapeDtypeStruct((SUBLANES, bsz * seq), F32),
                 jax.ShapeDtypeStruct((LANES, LANES), F32)]
    return in_specs, inputs, out_specs, out_shape, pltpu.VMEM((LANES, 1), F32)


def _mod_kernel(c_ref, w_ref, b_ref, o_ref):
    c = c_ref[...]
    ca = c * jax.nn.sigmoid(c)
    o_ref[...] = jnp.dot(ca, w_ref[...], preferred_element_type=F32,
                         precision=lax.Precision.HIGHEST) + b_ref[...]


def _modulation(c, w, b, bn=1024):
    nl, d, n = w.shape
    bsz = c.shape[0]
    return pl.pallas_call(
        _mod_kernel,
        grid=(nl, n // bn),
        in_specs=[pl.BlockSpec((bsz, d), lambda l, j: (0, 0)),
                  pl.BlockSpec((None, d, bn), lambda l, j: (l, 0, j)),
                  pl.BlockSpec((None, 1, bn), lambda l, j: (l, 0, j))],
        out_specs=pl.BlockSpec((None, bsz, bn), lambda l, j: (l, 0, j)),
        out_shape=jax.ShapeDtypeStruct((nl, bsz, n), F32),
        compiler_params=_params(("arbitrary", "arbitrary")),
        name="modulation",
    )(c, w, b.reshape(nl, 1, n))


def _conv_kernel(x_ref, g_ref, sh_ref, sc_ref, gt_ref, win_ref, cw_ref, cb_ref, wout_ref, *rest, d, ts):
    route_in, (o_ref, hlin_ref, meta_ref, cnt_ref, carry_ref, run_ref) = rest[:6], rest[6:]

    @pl.when(pl.program_id(1) == 0)
    def _():
        carry_ref[...] = jnp.zeros_like(carry_ref)

    x = x_ref[...]
    h = _norm_mod(x, g_ref[...], sh_ref[...], sc_ref[...]).astype(BF16)
    c_gate = jnp.dot(h, win_ref[:, d:2 * d], preferred_element_type=F32)
    v = jnp.dot(h, win_ref[:, 2 * d:3 * d], preferred_element_type=F32)
    z = c_gate * v
    prev = carry_ref[...]
    row = lax.broadcasted_iota(jnp.int32, (ts, 1), 0)
    z1 = jnp.where(row == 0, prev[SUBLANES - 1:SUBLANES], pltpu.roll(z, 1, 0))
    z2 = jnp.where(row == 0, prev[SUBLANES - 2:SUBLANES - 1],
                   jnp.where(row == 1, prev[SUBLANES - 1:SUBLANES], pltpu.roll(z, 2, 0)))
    carry_ref[...] = z[ts - SUBLANES:ts]
    cw = cw_ref[...]
    zc = cw[0:1] * z2 + cw[1:2] * z1 + cw[2:3] * z + cb_ref[...]
    b_gate = jnp.dot(h, win_ref[:, 0:d], preferred_element_type=F32)
    y = jnp.dot((b_gate * zc).astype(BF16), wout_ref[...], preferred_element_type=F32)
    xn = x + gt_ref[...] * y
    o_ref[...] = xn
    _route_tile(xn, *route_in, hlin_ref, meta_ref, cnt_ref, run_ref)


def _conv_layer(x, mod, l, g, w_in, cw, cb, w_out, route_args, ts=512):
    bsz, seq, d = x.shape
    row = lambda i: pl.BlockSpec((None, 1, d), lambda b, s, i=i: ((l * bsz + b) * 6 + i, 0, 0))
    const2 = lambda shape: pl.BlockSpec(shape, lambda b, s: (0, 0))
    xspec = pl.BlockSpec((None, ts, d), lambda b, s: (b, s, 0))
    r_in, r_args, r_out, r_shape, r_scratch = _route_io(mod, l, bsz, seq, d, ts, *route_args)
    return pl.pallas_call(
        functools.partial(_conv_kernel, d=d, ts=ts),
        grid=(bsz, seq // ts),
        in_specs=[xspec, const2((1, d)), row(0), row(1), row(2), const2((d, 3 * d)),
                  const2((SUBLANES, d)), const2((1, d)), const2((d, d))] + r_in,
        out_specs=[xspec] + r_out,
        out_shape=[jax.ShapeDtypeStruct(x.shape, F32)] + r_shape,
        scratch_shapes=[pltpu.VMEM((SUBLANES, d), F32), r_scratch],
        compiler_params=_params(("arbitrary", "arbitrary")),
        name="conv_layer",
    )(x, g.reshape(1, d), mod, mod, mod, w_in, jnp.pad(cw, ((0, SUBLANES - CONV_WIDTH), (0, 0))),
      cb.reshape(1, d), w_out, *r_args)


def _proj_kernel(x_ref, g_ref, sh_ref, sc_ref, w_ref, *o_refs, d, scale):
    h = _norm_mod(x_ref[...], g_ref[...], sh_ref[...], sc_ref[...]).astype(BF16)
    for j, o_ref in enumerate(o_refs):
        y = jnp.dot(h, w_ref[:, j * d:(j + 1) * d], preferred_element_type=F32)
        o_ref[...] = (y * scale).astype(o_ref.dtype)


def _proj(x, g, sh, sc, w, scale, ts=512):
    bsz, seq, d = x.shape
    n_out = w.shape[1] // d
    xspec = pl.BlockSpec((None, ts, d), lambda b, s: (b, s, 0))
    row = pl.BlockSpec((None, 1, d), lambda b, s: (b, 0, 0))
    return pl.pallas_call(
        functools.partial(_proj_kernel, d=d, scale=scale),
        grid=(bsz, seq // ts),
        in_specs=[xspec, pl.BlockSpec((1, d), lambda b, s: (0, 0)), row, row,
                  pl.BlockSpec(w.shape, lambda b, s: (0, 0))],
        out_specs=[xspec] * n_out,
        out_shape=[jax.ShapeDtypeStruct(x.shape, BF16)] * n_out,
        compiler_params=_params(("arbitrary", "arbitrary")),
        name="proj",
    )(x, g.reshape(1, d), sh, sc, w)


def _flash_kernel(slope_ref, q_ref, k_ref, v_ref, lq1_ref, lk1_ref, lq2_ref, lk2_ref, g_ref,
                  o_ref, qq_ref, kk_ref, vv_ref, s0_ref, s1_ref, m_ref, acc_ref, *, tq, tk, rc, hd, lambda_init):
    h = pl.program_id(1)
    g = pl.program_id(2)
    seq, vd = k_ref.shape

    @pl.when(g == 0)
    def _():
        pos = lax.broadcasted_iota(jnp.int32, (seq, vd), 0)
        col = lax.broadcasted_iota(jnp.int32, (seq, vd), 1)
        lo = pos % POS_SPLIT
        kk_ref[:, 0:vd] = k_ref[...]
        kk_ref[:, vd:2 * vd] = jnp.where(col == 0, pos - lo, jnp.where(col == 1, lo, 0)).astype(F32).astype(BF16)
        vv_ref[:, 0:vd] = v_ref[...]
        vv_ref[:, vd:2 * vd] = jnp.ones((seq, vd), BF16)

    for u in range(2):
        q = q_ref[u * tq:(u + 1) * tq, :]
        lane = lax.broadcasted_iota(jnp.int32, q.shape, 1)
        zero = jnp.zeros_like(q)
        slope_cols = jnp.where(lane < 2, slope_ref[h], 0.0).astype(BF16)
        qq_ref[u, 0:tq, 0:vd] = jnp.where(lane < hd, q, zero)
        qq_ref[u, tq:2 * tq, 0:vd] = jnp.where(lane >= hd, q, zero)
        qq_ref[u, 0:tq, vd:2 * vd] = slope_cols
        qq_ref[u, tq:2 * tq, vd:2 * vd] = slope_cols
    m_ref[...] = jnp.full_like(m_ref, NEG_INF)
    acc_ref[...] = jnp.zeros_like(acc_ref)

    def scores(u, j, s_ref):
        kj = kk_ref[pl.ds(pl.multiple_of(j * tk, tk), tk), :]
        s_ref[...] = lax.dot_general(qq_ref[u], kj, (((1,), (1,)), ((), ())), preferred_element_type=F32)

    def update(u, j, s_ref, masked):
        vj = vv_ref[pl.ds(pl.multiple_of(j * tk, tk), tk), :]
        for c in range(2 * tq // rc):
            rows = pl.ds(c * rc, rc)
            s = s_ref[rows, :]
            if masked:
                qpos = lax.broadcasted_iota(jnp.int32, (rc, tk), 0) + (c * rc) % tq
                kpos = lax.broadcasted_iota(jnp.int32, (rc, tk), 1)
                s = jnp.where(qpos >= kpos, s, NEG_INF)
            m_prev = m_ref[u, rows, :]
            m_new = jnp.maximum(m_prev, jnp.max(s, axis=-1, keepdims=True))
            alpha = jnp.exp(m_prev - m_new)
            p = jnp.exp(s - jnp.tile(m_new, (1, tk // LANES)))
            pv = jnp.dot(p.astype(BF16), vj, preferred_element_type=F32)
            acc_ref[u, rows, :] = jnp.tile(alpha, (1, 2 * vd // LANES)) * acc_ref[u, rows, :] + pv
            m_ref[u, rows, :] = m_new

    def finalize(u):
        lam = (jnp.exp(jnp.sum(lq1_ref[...] * lk1_ref[...], keepdims=True))
               - jnp.exp(jnp.sum(lq2_ref[...] * lk2_ref[...], keepdims=True)) + lambda_init)
        acc = acc_ref[u]
        o12 = acc[:, 0:vd] / acc[:, vd:2 * vd]
        o = o12[0:tq] - lam * o12[tq:2 * tq]
        on = o * lax.rsqrt(jnp.mean(o * o, axis=-1, keepdims=True) + EPS) * g_ref[...]
        o_ref[u * tq:(u + 1) * tq, :] = (on * (1.0 - lambda_init)).astype(o_ref.dtype)

    def pairs(u, sa_ref, sb_ref):
        def body(i, carry):
            j = 2 * i
            scores(u, j + 1, sb_ref)
            update(u, j, sa_ref, False)
            scores(u, j + 2, sa_ref)
            update(u, j + 1, sb_ref, False)
            return carry
        lax.fori_loop(0, g, body, 0)

    scores(0, 0, s0_ref)
    pairs(0, s0_ref, s1_ref)
    scores(1, 0, s1_ref)
    update(0, 2 * g, s0_ref, True)
    pairs(1, s1_ref, s0_ref)
    scores(1, 2 * g + 1, s0_ref)
    update(1, 2 * g, s1_ref, False)
    finalize(0)
    update(1, 2 * g + 1, s0_ref, True)
    finalize(1)


def _diff_attention(q, k, v, lq1, lk1, lq2, lk2, subln_g, lambda_init, tq=512, tk=512, rc=512):
    bsz, seq, d = q.shape
    vd = d // N_HEADS
    hd = vd // 2
    assert vd == LANES and seq <= POS_SPLIT * 256 and 8 % N_HEADS == 0 and tq == tk and seq % (2 * tq) == 0
    slopes = jnp.exp2(-8.0 * (jnp.arange(N_HEADS, dtype=F32) + 1.0) / N_HEADS)
    vec = lambda n: pl.BlockSpec((1, n), lambda b, h, i, sl: (0, 0))
    kvspec = pl.BlockSpec((None, seq, vd), lambda b, h, i, sl: (b, 0, h))
    qspec = pl.BlockSpec((None, 2 * tq, vd), lambda b, h, i, sl: (b, i, h))
    grid_spec = pltpu.PrefetchScalarGridSpec(
        num_scalar_prefetch=1,
        grid=(bsz, N_HEADS, seq // (2 * tq)),
        in_specs=[qspec, kvspec, kvspec, vec(hd), vec(hd), vec(hd), vec(hd), vec(vd)],
        out_specs=qspec,
        scratch_shapes=[pltpu.VMEM((2, 2 * tq, 2 * vd), BF16), pltpu.VMEM((seq, 2 * vd), BF16),
                        pltpu.VMEM((seq, 2 * vd), BF16), pltpu.VMEM((2 * tq, tk), F32), pltpu.VMEM((2 * tq, tk), F32),
                        pltpu.VMEM((2, 2 * tq, LANES), F32),
                        pltpu.VMEM((2, 2 * tq, 2 * vd), F32)],
    )
    return pl.pallas_call(
        functools.partial(_flash_kernel, tq=tq, tk=tk, rc=rc, hd=hd, lambda_init=lambda_init),
        grid_spec=grid_spec,
        out_shape=jax.ShapeDtypeStruct(q.shape, BF16),
        compiler_params=_params(("arbitrary", "arbitrary", "arbitrary")),
        name="diff_attention",
    )(slopes, q, k, v, lq1.reshape(1, hd), lk1.reshape(1, hd), lq2.reshape(1, hd), lk2.reshape(1, hd),
      subln_g.reshape(1, vd))


def _oproj_kernel(a_ref, w_ref, x_ref, gt_ref, *rest):
    route_in, (o_ref, hlin_ref, meta_ref, cnt_ref, run_ref) = rest[:6], rest[6:]
    y = jnp.dot(a_ref[...], w_ref[...], preferred_element_type=F32)
    xn = x_ref[...] + gt_ref[...] * y
    o_ref[...] = xn
    _route_tile(xn, *route_in, hlin_ref, meta_ref, cnt_ref, run_ref)


def _oproj(a, w, x, mod, l, route_args, ts=512):
    bsz, seq, d = x.shape
    xspec = pl.BlockSpec((None, ts, d), lambda b, s: (b, s, 0))
    r_in, r_args, r_out, r_shape, r_scratch = _route_io(mod, l, bsz, seq, d, ts, *route_args)
    return pl.pallas_call(
        _oproj_kernel,
        grid=(bsz, seq // ts),
        in_specs=[xspec, pl.BlockSpec((d, d), lambda b, s: (0, 0)), xspec,
                  pl.BlockSpec((None, 1, d), lambda b, s: ((l * bsz + b) * 6 + 2, 0, 0))] + r_in,
        out_specs=[xspec] + r_out,
        out_shape=[jax.ShapeDtypeStruct(x.shape, F32)] + r_shape,
        scratch_shapes=[r_scratch],
        compiler_params=_params(("arbitrary", "arbitrary")),
        name="attn_out",
    )(a, w, x, mod, *r_args)


def _row_dma_loop(n_rows, copy_fn):
    def body(i, carry):
        for u in range(DMA_UNROLL):
            copy_fn(i * DMA_UNROLL + u).start(priority=u % 2)
        return carry
    lax.fori_loop(0, n_rows // DMA_UNROLL, body, 0)


def _dispatch_kernel(pos_ref, ztile_ref, hlin_ref, hs_ref, zero_ref, sem, zsem, *, nch, chunk, tm):
    def rows(ref, r, n=1):
        return ref.at[pl.ds(pl.multiple_of(r * nch, nch), n * nch)]

    c = pl.program_id(0)

    @pl.when(c == 0)
    def _():
        zero_ref[...] = jnp.zeros_like(zero_ref)
        fill = lambda k: pltpu.make_async_copy(zero_ref, rows(hs_ref, ztile_ref[k] * tm, tm), zsem)
        for k in range(ztile_ref.shape[0]):
            @pl.when(ztile_ref[k] >= 0)
            def _(k=k):
                fill(k).start()
        for k in range(ztile_ref.shape[0]):
            @pl.when(ztile_ref[k] >= 0)
            def _(k=k):
                fill(k).wait()

    _row_dma_loop(chunk, lambda r: pltpu.make_async_copy(
        rows(hlin_ref, r), rows(hs_ref, pos_ref[c * chunk + r]), sem))
    pltpu.make_async_copy(hlin_ref, rows(hs_ref, 0, chunk), sem).wait()


def _dispatch(pos, ztile, hlin, n_rows, nch, tm, chunk=2048):
    t_all = pos.shape[0]
    grid_spec = pltpu.PrefetchScalarGridSpec(
        num_scalar_prefetch=2, grid=(t_all // chunk,),
        in_specs=[pl.BlockSpec((chunk * nch, LANES), lambda c, pos, zt: (c, 0))],
        out_specs=pl.BlockSpec(memory_space=pl.ANY),
        scratch_shapes=[pltpu.VMEM((tm * nch, LANES), F32), pltpu.SemaphoreType.DMA(()),
                        pltpu.SemaphoreType.DMA(())])
    return pl.pallas_call(
        functools.partial(_dispatch_kernel, nch=nch, chunk=chunk, tm=tm),
        grid_spec=grid_spec,
        out_shape=jax.ShapeDtypeStruct((n_rows * nch, LANES), F32),
        compiler_params=_params(("arbitrary",)),
        name="moe_dispatch",
    )(pos, ztile, hlin)


def _ffn_kernel(ta_ref, tb_ref, tg_ref, tv_ref, tblk_ref, tnew_ref, hs_ref, rw_ref, rb_ref, *rest, tm, nch):
    del tblk_ref
    w_f32, ys_ref, w_bf16 = rest[:6], rest[6], rest[7:]
    w1a_ref, w3a_ref, w2a_ref, w1b_ref, w3b_ref, w2b_ref = w_bf16
    i = pl.program_id(0)

    @pl.when(tv_ref[i] == 0)
    def _():
        ys_ref[...] = jnp.zeros_like(ys_ref)

    @pl.when(tnew_ref[i] == 1)
    def _():
        for src, dst in zip(w_f32, w_bf16):
            dst[...] = src[...].astype(BF16)

    @pl.when(tv_ref[i] == 1)
    def _():
        x = jnp.concatenate([hs_ref[pl.ds(j, tm, stride=nch), :] for j in range(nch)], axis=-1).astype(BF16)
        logits = jnp.dot(x, rw_ref[...], preferred_element_type=F32) + rb_ref[...]
        lane = lax.broadcasted_iota(jnp.int32, logits.shape, 1)
        pick = lambda col, val: jnp.sum(jnp.where(lane == col, val, 0.0), axis=-1, keepdims=True)
        gl = jnp.where(lane < N_GROUPS, logits, NEG_INF)
        eg = jnp.exp(gl - jnp.max(gl, axis=-1, keepdims=True))
        pg = pick(tg_ref[i], eg) / jnp.sum(eg, axis=-1, keepdims=True)
        la = pick(N_GROUPS + ta_ref[i], logits)
        lb = pick(N_GROUPS + tb_ref[i], logits)
        mx = jnp.maximum(la, lb)
        ea = jnp.exp(la - mx)
        eb = jnp.exp(lb - mx)
        wa = pg * (ea / (ea + eb))
        wb = pg * (eb / (ea + eb))

        def ffn(w1_ref, w3_ref, w2_ref):
            a = jnp.dot(x, w1_ref[...], preferred_element_type=F32)
            b = jnp.dot(x, w3_ref[...], preferred_element_type=F32)
            hid = (a * jax.nn.sigmoid(a)) * b
            return jnp.dot(hid.astype(BF16), w2_ref[...], preferred_element_type=F32)

        y = wa * ffn(w1a_ref, w3a_ref, w2a_ref) + wb * ffn(w1b_ref, w3b_ref, w2b_ref)
        for j in range(nch):
            ys_ref[pl.ds(j, tm, stride=nch), :] = y[:, j * LANES:(j + 1) * LANES]


def _ffn(tiles, hs, rw, rb, l, w1, w3, w2, tm, nch):
    ta = tiles[0]
    n_tiles = ta.shape[0]
    d, de = w1.shape[2:]
    rows = pl.BlockSpec((tm * nch, LANES), lambda i, ta, tb, tg, tv, tblk, tnew: (tblk[i], 0))
    const = lambda shape: pl.BlockSpec(shape, lambda i, *_: (0, 0))
    wa = lambda shape: pl.BlockSpec((None, None) + shape, lambda i, ta, tb, tg, tv, tblk, tnew: (l, ta[i], 0, 0))
    wb = lambda shape: pl.BlockSpec((None, None) + shape, lambda i, ta, tb, tg, tv, tblk, tnew: (l, tb[i], 0, 0))
    grid_spec = pltpu.PrefetchScalarGridSpec(
        num_scalar_prefetch=6, grid=(n_tiles,),
        in_specs=[rows, const((d, LANES)), const((1, LANES)),
                  wa((d, de)), wa((d, de)), wa((de, d)), wb((d, de)), wb((d, de)), wb((de, d))],
        out_specs=pl.BlockSpec((tm * nch, LANES), lambda i, *_: (i, 0)),
        scratch_shapes=[pltpu.VMEM(s, BF16) for s in ((d, de), (d, de), (de, d)) * 2])
    return pl.pallas_call(
        functools.partial(_ffn_kernel, tm=tm, nch=nch),
        grid_spec=grid_spec,
        out_shape=jax.ShapeDtypeStruct(hs.shape, F32),
        compiler_params=pltpu.CompilerParams(dimension_semantics=("arbitrary",), vmem_limit_bytes=FFN_VMEM_LIMIT),
        name="moe_ffn",
    )(*tiles, hs, rw, rb, w1, w3, w2, w1, w3, w2)


def _combine_kernel(pos_ref, ys_ref, x_ref, gt_ref, fg_ref, o_ref, buf_ref, sem, *, tg, nch, final):
    i = pl.program_id(0)
    n = pl.num_programs(0)

    def rows(ref, r, k=1):
        return ref.at[pl.ds(pl.multiple_of(r * nch, nch), k * nch)]

    def fetch(tile, slot):
        _row_dma_loop(tg, lambda r: pltpu.make_async_copy(
            rows(ys_ref, pos_ref[tile * tg + r]), rows(buf_ref, slot * tg + r), sem.at[slot]))

    @pl.when(i == 0)
    def _():
        fetch(0, 0)

    @pl.when(i + 1 < n)
    def _():
        fetch(i + 1, (i + 1) % 2)

    slot = i % 2
    pltpu.make_async_copy(rows(ys_ref, 0, tg), rows(buf_ref, slot * tg, tg), sem.at[slot]).wait()
    base = slot * (tg * nch)
    y = jnp.concatenate([buf_ref[pl.ds(base + j, tg, stride=nch), :] for j in range(nch)], axis=-1)
    xn = x_ref[...] + gt_ref[...] * y
    if final:
        xn = (xn * lax.rsqrt(jnp.mean(xn * xn, axis=-1, keepdims=True) + EPS)) * fg_ref[...]
    o_ref[...] = xn


def _combine(pos, ys, x2, mod, l, bsz, seq, final_g, final, nch, tg=512):
    t_all, d = x2.shape
    tile = pl.BlockSpec((tg, d), lambda t, pos: (t, 0))
    grid_spec = pltpu.PrefetchScalarGridSpec(
        num_scalar_prefetch=1, grid=(t_all // tg,),
        in_specs=[pl.BlockSpec(memory_space=pl.ANY), tile,
                  pl.BlockSpec((None, 1, d), lambda t, pos: ((l * bsz + (t * tg) // seq) * 6 + 5, 0, 0)),
                  pl.BlockSpec((1, d), lambda t, pos: (0, 0))],
        out_specs=tile,
        scratch_shapes=[pltpu.VMEM((2 * tg * nch, LANES), F32), pltpu.SemaphoreType.DMA((2,))])
    return pl.pallas_call(
        functools.partial(_combine_kernel, tg=tg, nch=nch, final=final),
        grid_spec=grid_spec,
        out_shape=jax.ShapeDtypeStruct(x2.shape, F32),
        compiler_params=_params(("arbitrary",)),
        name="moe_combine",
    )(pos, ys, x2, mod, final_g.reshape(1, d))


def _moe(x, routed, mod, l, rw, rb, w1, w3, w2, final_g, final, tm=512):
    bsz, seq, d = x.shape
    t_all = bsz * seq
    nch = d // LANES
    x2 = x.reshape(t_all, d)
    hlin, meta, cnt = routed

    n_tiles = t_all // tm + N_BUCKETS
    bucket = meta[0].astype(jnp.int32)
    rank = (meta[1] * RANK_SPLIT + meta[2]).astype(jnp.int32)
    count = cnt[:N_BUCKETS, 0].astype(jnp.int32)
    padded = (count + tm - 1) // tm * tm
    ends = jnp.cumsum(padded)
    starts = ends - padded
    buckets = jnp.arange(N_BUCKETS, dtype=jnp.int32)
    pos = rank + jnp.sum(jnp.where(bucket[:, None] == buckets, starts, 0), axis=-1)
    n_valid = ends[-1] // tm
    tile_id = jnp.arange(n_tiles, dtype=jnp.int32)
    tblk = jnp.minimum(tile_id, n_valid - 1)
    tbucket = jnp.sum((ends <= (tblk * tm)[:, None]).astype(jnp.int32), axis=-1)
    pair = tbucket % N_PAIRS
    tgrp = tbucket // N_PAIRS
    pick = lambda table: jnp.sum(jnp.where(pair[:, None] == jnp.arange(N_PAIRS), jnp.asarray(table, jnp.int32), 0), -1)
    ta = tgrp * EXPERTS_PER_GROUP + pick(PAIR_LO)
    tb = tgrp * EXPERTS_PER_GROUP + pick(PAIR_HI)
    tv = (tile_id < n_valid).astype(jnp.int32)
    tnew = jnp.concatenate([jnp.ones((1,), jnp.int32), (tbucket[1:] != tbucket[:-1]).astype(jnp.int32)])
    last = jnp.where(padded > 0, ends // tm - 1, -1)
    tail = jnp.where(n_valid + buckets < n_tiles, n_valid + buckets, -1)
    ztile = jnp.concatenate([last, tail]).astype(jnp.int32)

    hs = _dispatch(pos, ztile, hlin, n_tiles * tm, nch, tm)
    ys = _ffn((ta, tb, tgrp, tv, tblk, tnew), hs, rw, rb, l, w1, w3, w2, tm, nch)
    out = _combine(pos, ys, x2, mod, l, bsz, seq, final_g, final, nch)
    return out.reshape(bsz, seq, d)


def kernel(x, c, mod_w, mod_b, norm_mix_g, norm_ffn_g, conv_in_w, conv_w, conv_b, conv_out_w, kv_mod_w, kv_mod_b, kv_norm_g, kv_w, q_w, lam_q1, lam_k1, lam_q2, lam_k2, subln_g, o_w, router_group_w, router_group_b, router_exp_w, router_exp_b, exp_w1, exp_w3, exp_w2, final_norm_g):
    bsz, seq, d = x.shape
    depth = mod_w.shape[0]
    n_a = conv_in_w.shape[0]
    hd = d // (2 * N_HEADS)

    mod = _modulation(c, mod_w, mod_b).reshape(depth * bsz * 6, 1, d)
    kvm = _modulation(c, kv_mod_w[None], kv_mod_b[None]).reshape(bsz, 2, 1, d)
    kv_sh, kv_sc = kvm[:, 0], kvm[:, 1]

    pad = LANES - N_GROUPS - N_EXPERTS
    rw = jnp.pad(jnp.concatenate([router_group_w, router_exp_w], axis=-1), ((0, 0), (0, 0), (0, pad))).astype(BF16)
    rb = jnp.pad(jnp.concatenate([router_group_b, router_exp_b], axis=-1), ((0, 0), (0, pad)))[:, None, :]

    k = v = None
    for l in range(depth):
        if l == n_a:
            k, v = _proj(x, kv_norm_g, kv_sh, kv_sc, kv_w.astype(BF16), 1.0)
        route_args = (norm_ffn_g[l], rw[l], rb[l])
        if l < n_a:
            x, *routed = _conv_layer(x, mod, l, norm_mix_g[l], conv_in_w[l].astype(BF16), conv_w[l], conv_b[l],
                                     conv_out_w[l].astype(BF16), route_args)
        else:
            j = l - n_a
            lambda_init = 0.8 - 0.6 * float(np.exp(-0.3 * l))
            m4 = mod.reshape(depth, bsz, 6, 1, d)
            (q,) = _proj(x, norm_mix_g[l], m4[l, :, 0], m4[l, :, 1], q_w[j].astype(BF16), hd ** -0.5)
            a = _diff_attention(q, k, v, lam_q1[j], lam_k1[j], lam_q2[j], lam_k2[j], subln_g[j], lambda_init)
            x, *routed = _oproj(a, o_w[j].astype(BF16), x, mod, l, route_args)
        x = _moe(x, routed, mod, l, rw[l], rb[l], exp_w1, exp_w3, exp_w2, final_norm_g, final=(l == depth - 1))
    return x
```

```python
import functools

import jax
import jax.numpy as jnp
import numpy as np
from jax import lax
from jax.experimental import pallas as pl
from jax.experimental.pallas import tpu as pltpu

N_HEADS = 8
N_GROUPS = 4
EXPERTS_PER_GROUP = 4
N_EXPERTS = N_GROUPS * EXPERTS_PER_GROUP
CONV_WIDTH = 3
EPS = 1e-6
NEG_INF = -1e30
LANES = 128
SUBLANES = 8
LOG2E = 1.4426950408889634
POS_SPLIT = 128
VMEM_LIMIT = 48 * 1024 * 1024
FFN_VMEM_LIMIT = 56 * 1024 * 1024

F32 = jnp.float32
BF16 = jnp.bfloat16


def _params(sem):
    return pltpu.CompilerParams(dimension_semantics=sem, vmem_limit_bytes=VMEM_LIMIT)


def _norm_mod(x, g, sh, sc):
    ms = jnp.mean(x * x, axis=-1, keepdims=True)
    y = (x * lax.rsqrt(ms + EPS)) * g
    return y * (1.0 + sc) + sh


PAIR_LO = (0, 0, 0, 1, 1, 2)
PAIR_HI = (1, 2, 3, 2, 3, 3)
N_PAIRS = len(PAIR_LO)
N_BUCKETS = N_GROUPS * N_PAIRS
DMA_UNROLL = 8
RANK_SPLIT = 256


def _route_tile(x, g_ref, sh_ref, sc_ref, rw_ref, rb_ref, tri_ref, hlin_ref, meta_ref, cnt_ref, run_ref):
    ts, d = x.shape
    nch = d // LANES

    @pl.when((pl.program_id(0) == 0) & (pl.program_id(1) == 0))
    def _():
        run_ref[...] = jnp.zeros_like(run_ref)

    h = _norm_mod(x, g_ref[...], sh_ref[...], sc_ref[...]).astype(BF16)
    for j in range(nch):
        hlin_ref[pl.ds(j, ts, stride=nch), :] = h[:, j * LANES:(j + 1) * LANES].astype(F32)
    lt = lax.dot_general(rw_ref[...], h, (((1,), (1,)), ((), ())), preferred_element_type=F32) + rb_ref[...]
    row = lambda r: lt[r:r + 1, :]

    def first_argmax(vals):
        best, idx = vals[0], jnp.zeros_like(vals[0])
        for k in range(1, len(vals)):
            better = vals[k] > best
            idx = jnp.where(better, float(k), idx)
            best = jnp.where(better, vals[k], best)
        return idx

    g_sel = first_argmax([row(g) for g in range(N_GROUPS)])
    e = []
    for k in range(EXPERTS_PER_GROUP):
        ek = row(N_GROUPS + k)
        for g in range(1, N_GROUPS):
            ek = jnp.where(g_sel == float(g), row(N_GROUPS + EXPERTS_PER_GROUP * g + k), ek)
        e.append(ek)
    i1 = first_argmax(e)
    i2 = first_argmax([jnp.where(i1 == float(k), NEG_INF, e[k]) for k in range(EXPERTS_PER_GROUP)])
    a = jnp.minimum(i1, i2)
    b = jnp.maximum(i1, i2)
    bucket = N_PAIRS * g_sel + (a * (7.0 - a) * 0.5 + b - a - 1.0)
    sub = lax.broadcasted_iota(jnp.int32, lt.shape, 0).astype(F32)
    onehot = jnp.where(sub == bucket, 1.0, 0.0)
    before = jnp.dot(onehot.astype(BF16), tri_ref[...], preferred_element_type=F32) + run_ref[...]
    rank = jnp.sum(onehot * before, axis=0, keepdims=True)
    rank_hi = jnp.floor(rank * (1.0 / RANK_SPLIT))
    sub8 = lax.broadcasted_iota(jnp.int32, meta_ref.shape, 0)
    meta_ref[...] = jnp.where(sub8 == 0, bucket, jnp.where(sub8 == 1, rank_hi,
                                                           jnp.where(sub8 == 2, rank - RANK_SPLIT * rank_hi, 0.0)))
    run = run_ref[...] + jnp.sum(onehot, axis=1, keepdims=True)
    run_ref[...] = run
    cnt_ref[...] = jnp.broadcast_to(run, cnt_ref.shape)


def _route_io(mod, l, bsz, seq, d, ts, g, rw, rb):
    assert bsz * seq <= RANK_SPLIT * RANK_SPLIT
    nch = d // LANES
    nt = seq // ts
    row = lambda i: pl.BlockSpec((None, 1, d), lambda b, s, i=i: ((l * bsz + b) * 6 + i, 0, 0))
    const = lambda shape: pl.BlockSpec(shape, lambda b, s: (0, 0))
    in_specs = [const((1, d)), row(3), row(4), const((LANES, d)), const((LANES, 1)), const((ts, ts))]
    inputs = [g.reshape(1, d), mod, mod, rw.T, rb.reshape(LANES, 1), jnp.triu(jnp.ones((ts, ts), BF16), 1)]
    out_specs = [pl.BlockSpec((ts * nch, LANES), lambda b, s: (b * nt + s, 0)),
                 pl.BlockSpec((SUBLANES, ts), lambda b, s: (0, b * nt + s)), const((LANES, LANES))]
    out_shape = [jax.ShapeDtypeStruct((bsz * seq * nch, LANES), F32), jax.ShapeDtypeStruct((SUBLANES, bsz * seq), F32),
                 jax.ShapeDtypeStruct((LANES, LANES), F32)]
    return in_specs, inputs, out_specs, out_shape, pltpu.VMEM((LANES, 1), F32)


def _mod_kernel(c_ref, w_ref, b_ref, o_ref):
    c = c_ref[...]
    ca = c * jax.nn.sigmoid(c)
    o_ref[...] = jnp.dot(ca, w_ref[...], preferred_element_type=F32,
                         precision=lax.Precision.HIGHEST) + b_ref[...]


def _modulation(c, w, b, bn=1024):
    nl, d, n = w.shape
    bsz = c.shape[0]
    return pl.pallas_call(
        _mod_kernel,
        grid=(nl, n // bn),
        in_specs=[pl.BlockSpec((bsz, d), lambda l, j: (0, 0)),
                  pl.BlockSpec((None, d, bn), lambda l, j: (l, 0, j)),
                  pl.BlockSpec((None, 1, bn), lambda l, j: (l, 0, j))],
        out_specs=pl.BlockSpec((None, bsz, bn), lambda l, j: (l, 0, j)),
        out_shape=jax.ShapeDtypeStruct((nl, bsz, n), F32),
        compiler_params=_params(("arbitrary", "arbitrary")),
        name="modulation",
    )(c, w, b.reshape(nl, 1, n))


def _conv_kernel(x_ref, g_ref, sh_ref, sc_ref, gt_ref, win_ref, cw_ref, cb_ref, wout_ref, *rest, d, ts):
    route_in, (o_ref, hlin_ref, meta_ref, cnt_ref, carry_ref, run_ref) = rest[:6], rest[6:]

    @pl.when(pl.program_id(1) == 0)
    def _():
        carry_ref[...] = jnp.zeros_like(carry_ref)

    x = x_ref[...]
    h = _norm_mod(x, g_ref[...], sh_ref[...], sc_ref[...]).astype(BF16)
    c_gate = jnp.dot(h, win_ref[:, d:2 * d], preferred_element_type=F32)
    v = jnp.dot(h, win_ref[:, 2 * d:3 * d], preferred_element_type=F32)
    z = c_gate * v
    prev = carry_ref[...]
    row = lax.broadcasted_iota(jnp.int32, (ts, 1), 0)
    z1 = jnp.where(row == 0, prev[SUBLANES - 1:SUBLANES], pltpu.roll(z, 1, 0))
    z2 = jnp.where(row == 0, prev[SUBLANES - 2:SUBLANES - 1],
                   jnp.where(row == 1, prev[SUBLANES - 1:SUBLANES], pltpu.roll(z, 2, 0)))
    carry_ref[...] = z[ts - SUBLANES:ts]
    cw = cw_ref[...]
    zc = cw[0:1] * z2 + cw[1:2] * z1 + cw[2:3] * z + cb_ref[...]
    b_gate = jnp.dot(h, win_ref[:, 0:d], preferred_element_type=F32)
    y = jnp.dot((b_gate * zc).astype(BF16), wout_ref[...], preferred_element_type=F32)
    xn = x + gt_ref[...] * y
    o_ref[...] = xn
    _route_tile(xn, *route_in, hlin_ref, meta_ref, cnt_ref, run_ref)


def _conv_layer(x, mod, l, g, w_in, cw, cb, w_out, route_args, ts=512):
    bsz, seq, d = x.shape
    row = lambda i: pl.BlockSpec((None, 1, d), lambda b, s, i=i: ((l * bsz + b) * 6 + i, 0, 0))
    const2 = lambda shape: pl.BlockSpec(shape, lambda b, s: (0, 0))
    xspec = pl.BlockSpec((None, ts, d), lambda b, s: (b, s, 0))
    r_in, r_args, r_out, r_shape, r_scratch = _route_io(mod, l, bsz, seq, d, ts, *route_args)
    return pl.pallas_call(
        functools.partial(_conv_kernel, d=d, ts=ts),
        grid=(bsz, seq // ts),
        in_specs=[xspec, const2((1, d)), row(0), row(1), row(2), const2((d, 3 * d)),
                  const2((SUBLANES, d)), const2((1, d)), const2((d, d))] + r_in,
        out_specs=[xspec] + r_out,
        out_shape=[jax.ShapeDtypeStruct(x.shape, F32)] + r_shape,
        scratch_shapes=[pltpu.VMEM((SUBLANES, d), F32), r_scratch],
        compiler_params=_params(("arbitrary", "arbitrary")),
        name="conv_layer",
    )(x, g.reshape(1, d), mod, mod, mod, w_in, jnp.pad(cw, ((0, SUBLANES - CONV_WIDTH), (0, 0))),
      cb.reshape(1, d), w_out, *r_args)


def _proj_kernel(x_ref, g_ref, sh_ref, sc_ref, w_ref, *o_refs, d, scale):
    h = _norm_mod(x_ref[...], g_ref[...], sh_ref[...], sc_ref[...]).astype(BF16)
    for j, o_ref in enumerate(o_refs):
        y = jnp.dot(h, w_ref[:, j * d:(j + 1) * d], preferred_element_type=F32)
        o_ref[...] = (y * scale).astype(o_ref.dtype)


def _proj(x, g, sh, sc, w, scale, ts=512):
    bsz, seq, d = x.shape
    n_out = w.shape[1] // d
    xspec = pl.BlockSpec((None, ts, d), lambda b, s: (b, s, 0))
    row = pl.BlockSpec((None, 1, d), lambda b, s: (b, 0, 0))
    return pl.pallas_call(
        functools.partial(_proj_kernel, d=d, scale=scale),
        grid=(bsz, seq // ts),
        in_specs=[xspec, pl.BlockSpec((1, d), lambda b, s: (0, 0)), row, row,
                  pl.BlockSpec(w.shape, lambda b, s: (0, 0))],
        out_specs=[xspec] * n_out,
        out_shape=[jax.ShapeDtypeStruct(x.shape, BF16)] * n_out,
        compiler_params=_params(("arbitrary", "arbitrary")),
        name="proj",
    )(x, g.reshape(1, d), sh, sc, w)


def _flash_kernel(slope_ref, q_ref, k_ref, v_ref, lq1_ref, lk1_ref, lq2_ref, lk2_ref, g_ref,
                  o_ref, qq_ref, kk_ref, vv_ref, s0_ref, s1_ref, m_ref, acc_ref, *, tq, tk, rc, hd, lambda_init):
    h = pl.program_id(1)
    g = pl.program_id(2)
    seq, vd = k_ref.shape

    @pl.when((pl.program_id(0) == 0) & (h == 0) & (g == 0))
    def _():
        pos = lax.broadcasted_iota(jnp.int32, (seq, vd), 0)
        col = lax.broadcasted_iota(jnp.int32, (seq, vd), 1)
        lo = pos % POS_SPLIT
        kk_ref[:, vd:2 * vd] = jnp.where(col < 3, pos - lo, jnp.where(col < 6, lo, 0)).astype(F32).astype(BF16)
        vv_ref[:, vd:2 * vd] = jnp.ones((seq, vd), BF16)

    @pl.when(g == 0)
    def _():
        kk_ref[:, 0:vd] = k_ref[...]
        vv_ref[:, 0:vd] = v_ref[...]

    lane = lax.broadcasted_iota(jnp.int32, (tq, vd), 1)
    c0 = jnp.full((tq, vd), slope_ref[h] * LOG2E, F32)
    c1 = c0.astype(BF16).astype(F32)
    c2 = (c0 - c1).astype(BF16).astype(F32)
    c3 = c0 - c1 - c2
    piece = lane % 3
    slope_cols = jnp.where(lane < 6, jnp.where(piece == 0, c1, jnp.where(piece == 1, c2, c3)), 0.0).astype(BF16)
    for u in range(2):
        q = q_ref[u * tq:(u + 1) * tq, :]
        zero = jnp.zeros_like(q)
        qq_ref[u, 0:tq, 0:vd] = jnp.where(lane < hd, q, zero)
        qq_ref[u, tq:2 * tq, 0:vd] = jnp.where(lane >= hd, q, zero)
        qq_ref[u, 0:tq, vd:2 * vd] = slope_cols
        qq_ref[u, tq:2 * tq, vd:2 * vd] = slope_cols
    m_ref[...] = jnp.full_like(m_ref, NEG_INF)
    acc_ref[...] = jnp.zeros_like(acc_ref)

    def scores(u, j, s_ref):
        kj = kk_ref[pl.ds(pl.multiple_of(j * tk, tk), tk), :]
        s_ref[...] = lax.dot_general(qq_ref[u], kj, (((1,), (1,)), ((), ())), preferred_element_type=F32)

    def update(u, j, s_ref, masked):
        k0 = pl.multiple_of(j * tk, tk)
        rcu = rc // 2 if masked else rc
        for c in range(2 * tq // rcu):
            rows = pl.ds(c * rcu, rcu)
            r0 = (c * rcu) % tq
            nk = min(tk, r0 + rcu) if masked else tk
            s = s_ref[rows, 0:nk]
            if masked:
                qpos = lax.broadcasted_iota(jnp.int32, (rcu, nk), 0) + r0
                kpos = lax.broadcasted_iota(jnp.int32, (rcu, nk), 1)
                s = jnp.where(qpos >= kpos, s, NEG_INF)
            m_prev = m_ref[u, rows, :]
            m_new = jnp.maximum(m_prev, jnp.max(s, axis=-1, keepdims=True))
            alpha = jnp.exp2(m_prev - m_new)
            p = jnp.exp2(s - jnp.tile(m_new, (1, nk // LANES)))
            pv = jnp.dot(p.astype(BF16), vv_ref[pl.ds(k0, nk), :], preferred_element_type=F32)
            acc_ref[u, rows, :] = jnp.tile(alpha, (1, 2 * vd // LANES)) * acc_ref[u, rows, :] + pv
            m_ref[u, rows, :] = m_new

    def finalize(u):
        lam = (jnp.exp(jnp.sum(lq1_ref[...] * lk1_ref[...], keepdims=True))
               - jnp.exp(jnp.sum(lq2_ref[...] * lk2_ref[...], keepdims=True)) + lambda_init)
        acc = acc_ref[u]
        o12 = acc[:, 0:vd] / acc[:, vd:2 * vd]
        o = o12[0:tq] - lam * o12[tq:2 * tq]
        on = o * lax.rsqrt(jnp.mean(o * o, axis=-1, keepdims=True) + EPS) * g_ref[...]
        o_ref[u * tq:(u + 1) * tq, :] = (on * (1.0 - lambda_init)).astype(o_ref.dtype)

    def pairs(u, sa_ref, sb_ref):
        def body(i, carry):
            j = 2 * i
            scores(u, j + 1, sb_ref)
            update(u, j, sa_ref, False)
            scores(u, j + 2, sa_ref)
            update(u, j + 1, sb_ref, False)
            return carry
        lax.fori_loop(0, g, body, 0)

    scores(0, 0, s0_ref)
    pairs(0, s0_ref, s1_ref)
    scores(1, 0, s1_ref)
    update(0, 2 * g, s0_ref, True)
    pairs(1, s1_ref, s0_ref)
    scores(1, 2 * g + 1, s0_ref)
    update(1, 2 * g, s1_ref, False)
    finalize(0)
    update(1, 2 * g + 1, s0_ref, True)
    finalize(1)


def _diff_attention(q, k, v, lq1, lk1, lq2, lk2, subln_g, lambda_init, tq=512, tk=512, rc=512):
    bsz, seq, d = q.shape
    vd = d // N_HEADS
    hd = vd // 2
    assert vd == LANES and seq <= POS_SPLIT * 256 and tq == tk and seq % (2 * tq) == 0
    slopes = jnp.exp2(-8.0 * (jnp.arange(N_HEADS, dtype=F32) + 1.0) / N_HEADS)
    vec = lambda n: pl.BlockSpec((1, n), lambda b, h, i, sl: (0, 0))
    kvspec = pl.BlockSpec((None, seq, vd), lambda b, h, i, sl: (b, 0, h))
    qspec = pl.BlockSpec((None, 2 * tq, vd), lambda b, h, i, sl: (b, i, h))
    grid_spec = pltpu.PrefetchScalarGridSpec(
        num_scalar_prefetch=1,
        grid=(bsz, N_HEADS, seq // (2 * tq)),
        in_specs=[qspec, kvspec, kvspec, vec(hd), vec(hd), vec(hd), vec(hd), vec(vd)],
        out_specs=qspec,
        scratch_shapes=[pltpu.VMEM((2, 2 * tq, 2 * vd), BF16), pltpu.VMEM((seq, 2 * vd), BF16),
                        pltpu.VMEM((seq, 2 * vd), BF16), pltpu.VMEM((2 * tq, tk), F32), pltpu.VMEM((2 * tq, tk), F32),
                        pltpu.VMEM((2, 2 * tq, LANES), F32),
                        pltpu.VMEM((2, 2 * tq, 2 * vd), F32)],
    )
    return pl.pallas_call(
        functools.partial(_flash_kernel, tq=tq, tk=tk, rc=rc, hd=hd, lambda_init=lambda_init),
        grid_spec=grid_spec,
        out_shape=jax.ShapeDtypeStruct(q.shape, BF16),
        compiler_params=_params(("arbitrary", "arbitrary", "arbitrary")),
        name="diff_attention",
    )(slopes, q, k, v, lq1.reshape(1, hd), lk1.reshape(1, hd), lq2.reshape(1, hd), lk2.reshape(1, hd),
      subln_g.reshape(1, vd))


def _oproj_kernel(a_ref, w_ref, x_ref, gt_ref, *rest):
    route_in, (o_ref, hlin_ref, meta_ref, cnt_ref, run_ref) = rest[:6], rest[6:]
    y = jnp.dot(a_ref[...], w_ref[...], preferred_element_type=F32)
    xn = x_ref[...] + gt_ref[...] * y
    o_ref[...] = xn
    _route_tile(xn, *route_in, hlin_ref, meta_ref, cnt_ref, run_ref)


def _oproj(a, w, x, mod, l, route_args, ts=512):
    bsz, seq, d = x.shape
    xspec = pl.BlockSpec((None, ts, d), lambda b, s: (b, s, 0))
    r_in, r_args, r_out, r_shape, r_scratch = _route_io(mod, l, bsz, seq, d, ts, *route_args)
    return pl.pallas_call(
        _oproj_kernel,
        grid=(bsz, seq // ts),
        in_specs=[xspec, pl.BlockSpec((d, d), lambda b, s: (0, 0)), xspec,
                  pl.BlockSpec((None, 1, d), lambda b, s: ((l * bsz + b) * 6 + 2, 0, 0))] + r_in,
        out_specs=[xspec] + r_out,
        out_shape=[jax.ShapeDtypeStruct(x.shape, F32)] + r_shape,
        scratch_shapes=[r_scratch],
        compiler_params=_params(("arbitrary", "arbitrary")),
        name="attn_out",
    )(a, w, x, mod, *r_args)


def _row_dma_loop(n_rows, copy_fn):
    def body(i, carry):
        for u in range(DMA_UNROLL):
            copy_fn(i * DMA_UNROLL + u).start(priority=u % 2)
        return carry
    lax.fori_loop(0, n_rows // DMA_UNROLL, body, 0)


def _dispatch_kernel(pos_ref, ztile_ref, hlin_ref, hs_ref, zero_ref, sem, zsem, *, nch, chunk, tm):
    def rows(ref, r, n=1):
        return ref.at[pl.ds(pl.multiple_of(r * nch, nch), n * nch)]

    c = pl.program_id(0)

    @pl.when(c == 0)
    def _():
        zero_ref[...] = jnp.zeros_like(zero_ref)
        fill = lambda k: pltpu.make_async_copy(zero_ref, rows(hs_ref, ztile_ref[k] * tm, tm), zsem)
        for k in range(ztile_ref.shape[0]):
            @pl.when(ztile_ref[k] >= 0)
            def _(k=k):
                fill(k).start()
        for k in range(ztile_ref.shape[0]):
            @pl.when(ztile_ref[k] >= 0)
            def _(k=k):
                fill(k).wait()

    _row_dma_loop(chunk, lambda r: pltpu.make_async_copy(
        rows(hlin_ref, r), rows(hs_ref, pos_ref[c * chunk + r]), sem))
    pltpu.make_async_copy(hlin_ref, rows(hs_ref, 0, chunk), sem).wait()


def _dispatch(pos, ztile, hlin, n_rows, nch, tm, chunk=2048):
    t_all = pos.shape[0]
    grid_spec = pltpu.PrefetchScalarGridSpec(
        num_scalar_prefetch=2, grid=(t_all // chunk,),
        in_specs=[pl.BlockSpec((chunk * nch, LANES), lambda c, pos, zt: (c, 0))],
        out_specs=pl.BlockSpec(memory_space=pl.ANY),
        scratch_shapes=[pltpu.VMEM((tm * nch, LANES), F32), pltpu.SemaphoreType.DMA(()),
                        pltpu.SemaphoreType.DMA(())])
    return pl.pallas_call(
        functools.partial(_dispatch_kernel, nch=nch, chunk=chunk, tm=tm),
        grid_spec=grid_spec,
        out_shape=jax.ShapeDtypeStruct((n_rows * nch, LANES), F32),
        compiler_params=_params(("arbitrary",)),
        name="moe_dispatch",
    )(pos, ztile, hlin)


def _ffn_kernel(ta_ref, tb_ref, tg_ref, tv_ref, tblk_ref, tnew_ref, hs_ref, rw_ref, rb_ref, *rest, tm, nch):
    del tblk_ref
    w_f32, ys_ref, w_bf16 = rest[:6], rest[6], rest[7:]
    w1a_ref, w3a_ref, w2a_ref, w1b_ref, w3b_ref, w2b_ref = w_bf16
    i = pl.program_id(0)

    @pl.when(tv_ref[i] == 0)
    def _():
        ys_ref[...] = jnp.zeros_like(ys_ref)

    @pl.when(tnew_ref[i] == 1)
    def _():
        for src, dst in zip(w_f32, w_bf16):
            dst[...] = src[...].astype(BF16)

    @pl.when(tv_ref[i] == 1)
    def _():
        x = jnp.concatenate([hs_ref[pl.ds(j, tm, stride=nch), :] for j in range(nch)], axis=-1).astype(BF16)
        logits = jnp.dot(x, rw_ref[...], preferred_element_type=F32) + rb_ref[...]
        lane = lax.broadcasted_iota(jnp.int32, logits.shape, 1)
        pick = lambda col, val: jnp.sum(jnp.where(lane == col, val, 0.0), axis=-1, keepdims=True)
        gl = jnp.where(lane < N_GROUPS, logits, NEG_INF)
        eg = jnp.exp(gl - jnp.max(gl, axis=-1, keepdims=True))
        pg = pick(tg_ref[i], eg) / jnp.sum(eg, axis=-1, keepdims=True)
        la = pick(N_GROUPS + ta_ref[i], logits)
        lb = pick(N_GROUPS + tb_ref[i], logits)
        mx = jnp.maximum(la, lb)
        ea = jnp.exp(la - mx)
        eb = jnp.exp(lb - mx)
        wa = pg * (ea / (ea + eb))
        wb = pg * (eb / (ea + eb))

        def ffn(w1_ref, w3_ref, w2_ref):
            a = jnp.dot(x, w1_ref[...], preferred_element_type=F32)
            b = jnp.dot(x, w3_ref[...], preferred_element_type=F32)
            hid = (a * jax.nn.sigmoid(a)) * b
            return jnp.dot(hid.astype(BF16), w2_ref[...], preferred_element_type=F32)

        y = wa * ffn(w1a_ref, w3a_ref, w2a_ref) + wb * ffn(w1b_ref, w3b_ref, w2b_ref)
        for j in range(nch):
            ys_ref[pl.ds(j, tm, stride=nch), :] = y[:, j * LANES:(j + 1) * LANES]


def _ffn(tiles, hs, rw, rb, l, w1, w3, w2, tm, nch):
    ta = tiles[0]
    n_tiles = ta.shape[0]
    d, de = w1.shape[2:]
    rows = pl.BlockSpec((tm * nch, LANES), lambda i, ta, tb, tg, tv, tblk, tnew: (tblk[i], 0))
    const = lambda shape: pl.BlockSpec(shape, lambda i, *_: (0, 0))
    wa = lambda shape: pl.BlockSpec((None, None) + shape, lambda i, ta, tb, tg, tv, tblk, tnew: (l, ta[i], 0, 0))
    wb = lambda shape: pl.BlockSpec((None, None) + shape, lambda i, ta, tb, tg, tv, tblk, tnew: (l, tb[i], 0, 0))
    grid_spec = pltpu.PrefetchScalarGridSpec(
        num_scalar_prefetch=6, grid=(n_tiles,),
        in_specs=[rows, const((d, LANES)), const((1, LANES)),
                  wa((d, de)), wa((d, de)), wa((de, d)), wb((d, de)), wb((d, de)), wb((de, d))],
        out_specs=pl.BlockSpec((tm * nch, LANES), lambda i, *_: (i, 0)),
        scratch_shapes=[pltpu.VMEM(s, BF16) for s in ((d, de), (d, de), (de, d)) * 2])
    return pl.pallas_call(
        functools.partial(_ffn_kernel, tm=tm, nch=nch),
        grid_spec=grid_spec,
        out_shape=jax.ShapeDtypeStruct(hs.shape, F32),
        compiler_params=pltpu.CompilerParams(dimension_semantics=("arbitrary",), vmem_limit_bytes=FFN_VMEM_LIMIT),
        name="moe_ffn",
    )(*tiles, hs, rw, rb, w1, w3, w2, w1, w3, w2)


def _combine_kernel(pos_ref, ys_ref, x_ref, gt_ref, fg_ref, o_ref, buf_ref, sem, *, tg, nch, final):
    i = pl.program_id(0)
    n = pl.num_programs(0)

    def rows(ref, r, k=1):
        return ref.at[pl.ds(pl.multiple_of(r * nch, nch), k * nch)]

    def fetch(tile, slot):
        _row_dma_loop(tg, lambda r: pltpu.make_async_copy(
            rows(ys_ref, pos_ref[tile * tg + r]), rows(buf_ref, slot * tg + r), sem.at[slot]))

    @pl.when(i == 0)
    def _():
        fetch(0, 0)

    @pl.when(i + 1 < n)
    def _():
        fetch(i + 1, (i + 1) % 2)

    slot = i % 2
    pltpu.make_async_copy(rows(ys_ref, 0, tg), rows(buf_ref, slot * tg, tg), sem.at[slot]).wait()
    base = slot * (tg * nch)
    y = jnp.concatenate([buf_ref[pl.ds(base + j, tg, stride=nch), :] for j in range(nch)], axis=-1)
    xn = x_ref[...] + gt_ref[...] * y
    if final:
        xn = (xn * lax.rsqrt(jnp.mean(xn * xn, axis=-1, keepdims=True) + EPS)) * fg_ref[...]
    o_ref[...] = xn


def _combine(pos, ys, x2, mod, l, bsz, seq, final_g, final, nch, tg=512):
    t_all, d = x2.shape
    tile = pl.BlockSpec((tg, d), lambda t, pos: (t, 0))
    grid_spec = pltpu.PrefetchScalarGridSpec(
        num_scalar_prefetch=1, grid=(t_all // tg,),
        in_specs=[pl.BlockSpec(memory_space=pl.ANY), tile,
                  pl.BlockSpec((None, 1, d), lambda t, pos: ((l * bsz + (t * tg) // seq) * 6 + 5, 0, 0)),
                  pl.BlockSpec((1, d), lambda t, pos: (0, 0))],
        out_specs=tile,
        scratch_shapes=[pltpu.VMEM((2 * tg * nch, LANES), F32), pltpu.SemaphoreType.DMA((2,))])
    return pl.pallas_call(
        functools.partial(_combine_kernel, tg=tg, nch=nch, final=final),
        grid_spec=grid_spec,
        out_shape=jax.ShapeDtypeStruct(x2.shape, F32),
        compiler_params=_params(("arbitrary",)),
        name="moe_combine",
    )(pos, ys, x2, mod, final_g.reshape(1, d))


def _moe(x, routed, mod, l, rw, rb, w1, w3, w2, final_g, final, tm=512):
    bsz, seq, d = x.shape
    t_all = bsz * seq
    nch = d // LANES
    x2 = x.reshape(t_all, d)
    hlin, meta, cnt = routed

    n_tiles = t_all // tm + N_BUCKETS
    bucket = meta[0].astype(jnp.int32)
    rank = (meta[1] * RANK_SPLIT + meta[2]).astype(jnp.int32)
    count = cnt[:N_BUCKETS, 0].astype(jnp.int32)
    padded = (count + tm - 1) // tm * tm
    ends = jnp.cumsum(padded)
    starts = ends - padded
    buckets = jnp.arange(N_BUCKETS, dtype=jnp.int32)
    pos = rank + jnp.sum(jnp.where(bucket[:, None] == buckets, starts, 0), axis=-1)
    n_valid = ends[-1] // tm
    tile_id = jnp.arange(n_tiles, dtype=jnp.int32)
    tblk = jnp.minimum(tile_id, n_valid - 1)
    tbucket = jnp.sum((ends <= (tblk * tm)[:, None]).astype(jnp.int32), axis=-1)
    pair = tbucket % N_PAIRS
    tgrp = tbucket // N_PAIRS
    pick = lambda table: jnp.sum(jnp.where(pair[:, None] == jnp.arange(N_PAIRS), jnp.asarray(table, jnp.int32), 0), -1)
    ta = tgrp * EXPERTS_PER_GROUP + pick(PAIR_LO)
    tb = tgrp * EXPERTS_PER_GROUP + pick(PAIR_HI)
    tv = (tile_id < n_valid).astype(jnp.int32)
    tnew = jnp.concatenate([jnp.ones((1,), jnp.int32), (tbucket[1:] != tbucket[:-1]).astype(jnp.int32)])
    last = jnp.where(padded > 0, ends // tm - 1, -1)
    tail = jnp.where(n_valid + buckets < n_tiles, n_valid + buckets, -1)
    ztile = jnp.concatenate([last, tail]).astype(jnp.int32)

    hs = _dispatch(pos, ztile, hlin, n_tiles * tm, nch, tm)
    ys = _ffn((ta, tb, tgrp, tv, tblk, tnew), hs, rw, rb, l, w1, w3, w2, tm, nch)
    out = _combine(pos, ys, x2, mod, l, bsz, seq, final_g, final, nch)
    return out.reshape(bsz, seq, d)


def kernel(x, c, mod_w, mod_b, norm_mix_g, norm_ffn_g, conv_in_w, conv_w, conv_b, conv_out_w, kv_mod_w, kv_mod_b, kv_norm_g, kv_w, q_w, lam_q1, lam_k1, lam_q2, lam_k2, subln_g, o_w, router_group_w, router_group_b, router_exp_w, router_exp_b, exp_w1, exp_w3, exp_w2, final_norm_g):
    bsz, seq, d = x.shape
    depth = mod_w.shape[0]
    n_a = conv_in_w.shape[0]
    hd = d // (2 * N_HEADS)

    mod = _modulation(c, mod_w, mod_b).reshape(depth * bsz * 6, 1, d)
    kvm = _modulation(c, kv_mod_w[None], kv_mod_b[None]).reshape(bsz, 2, 1, d)
    kv_sh, kv_sc = kvm[:, 0], kvm[:, 1]

    pad = LANES - N_GROUPS - N_EXPERTS
    rw = jnp.pad(jnp.concatenate([router_group_w, router_exp_w], axis=-1), ((0, 0), (0, 0), (0, pad))).astype(BF16)
    rb = jnp.pad(jnp.concatenate([router_group_b, router_exp_b], axis=-1), ((0, 0), (0, pad)))[:, None, :]

    k = v = None
    for l in range(depth):
        if l == n_a:
            k, v = _proj(x, kv_norm_g, kv_sh, kv_sc, kv_w.astype(BF16), 1.0)
        route_args = (norm_ffn_g[l], rw[l], rb[l])
        if l < n_a:
            x, *routed = _conv_layer(x, mod, l, norm_mix_g[l], conv_in_w[l].astype(BF16), conv_w[l], conv_b[l],
                                     conv_out_w[l].astype(BF16), route_args)
        else:
            j = l - n_a
            lambda_init = 0.8 - 0.6 * float(np.exp(-0.3 * l))
            m4 = mod.reshape(depth, bsz, 6, 1, d)
            (q,) = _proj(x, norm_mix_g[l], m4[l, :, 0], m4[l, :, 1], q_w[j].astype(BF16), hd ** -0.5 * LOG2E)
            a = _diff_attention(q, k, v, lam_q1[j], lam_k1[j], lam_q2[j], lam_k2[j], subln_g[j], lambda_init)
            x, *routed = _oproj(a, o_w[j].astype(BF16), x, mod, l, route_args)
        x = _moe(x, routed, mod, l, rw[l], rb[l], exp_w1, exp_w3, exp_w2, final_norm_g, final=(l == depth - 1))
    return x
```

```python
import functools

import jax
import jax.numpy as jnp
import numpy as np
from jax import lax
from jax.experimental import pallas as pl
from jax.experimental.pallas import tpu as pltpu

N_HEADS = 8
N_GROUPS = 4
EXPERTS_PER_GROUP = 4
N_EXPERTS = N_GROUPS * EXPERTS_PER_GROUP
CONV_WIDTH = 3
EPS = 1e-6
NEG_INF = -1e30
LANES = 128
SUBLANES = 8
LOG2E = 1.4426950408889634
POS_SPLIT = 128
VMEM_LIMIT = 48 * 1024 * 1024
FFN_VMEM_LIMIT = 56 * 1024 * 1024

F32 = jnp.float32
BF16 = jnp.bfloat16


def _params(sem):
    return pltpu.CompilerParams(dimension_semantics=sem, vmem_limit_bytes=VMEM_LIMIT)


def _norm_mod(x, g, sh, sc):
    ms = jnp.mean(x * x, axis=-1, keepdims=True)
    y = (x * lax.rsqrt(ms + EPS)) * g
    return y * (1.0 + sc) + sh


PAIR_LO = (0, 0, 0, 1, 1, 2)
PAIR_HI = (1, 2, 3, 2, 3, 3)
N_PAIRS = len(PAIR_LO)
N_BUCKETS = N_GROUPS * N_PAIRS
DMA_UNROLL = 16
ROUTE_PARTS = 1
RANK_SPLIT = 256


def _route_init(run_ref):
    @pl.when((pl.program_id(0) == 0) & (pl.program_id(1) == 0))
    def _():
        run_ref[...] = jnp.zeros_like(run_ref)


def _route_tile(x, row0, g_ref, sh_ref, sc_ref, rw_ref, rb_ref, tri_ref, hlin_ref, meta_ref, cnt_ref, run_ref):
    ts, d = x.shape
    nch = d // LANES
    h = _norm_mod(x, g_ref[...], sh_ref[...], sc_ref[...]).astype(BF16)
    for j in range(nch):
        hlin_ref[pl.ds(row0 * nch + j, ts, stride=nch), :] = h[:, j * LANES:(j + 1) * LANES].astype(F32)
    lt = lax.dot_general(rw_ref[...], h, (((1,), (1,)), ((), ())), preferred_element_type=F32) + rb_ref[...]
    row = lambda r: lt[r:r + 1, :]

    def first_argmax(vals):
        best, idx = vals[0], jnp.zeros_like(vals[0])
        for k in range(1, len(vals)):
            better = vals[k] > best
            idx = jnp.where(better, float(k), idx)
            best = jnp.where(better, vals[k], best)
        return idx

    g_sel = first_argmax([row(g) for g in range(N_GROUPS)])
    e = []
    for k in range(EXPERTS_PER_GROUP):
        ek = row(N_GROUPS + k)
        for g in range(1, N_GROUPS):
            ek = jnp.where(g_sel == float(g), row(N_GROUPS + EXPERTS_PER_GROUP * g + k), ek)
        e.append(ek)
    i1 = first_argmax(e)
    i2 = first_argmax([jnp.where(i1 == float(k), NEG_INF, e[k]) for k in range(EXPERTS_PER_GROUP)])
    a = jnp.minimum(i1, i2)
    b = jnp.maximum(i1, i2)
    bucket = N_PAIRS * g_sel + (a * (7.0 - a) * 0.5 + b - a - 1.0)
    sub = lax.broadcasted_iota(jnp.int32, lt.shape, 0).astype(F32)
    onehot = jnp.where(sub == bucket, 1.0, 0.0)
    before = jnp.dot(onehot.astype(BF16), tri_ref[...], preferred_element_type=F32) + run_ref[...]
    rank = jnp.sum(onehot * before, axis=0, keepdims=True)
    rank_hi = jnp.floor(rank * (1.0 / RANK_SPLIT))
    sub8 = lax.broadcasted_iota(jnp.int32, (SUBLANES, ts), 0)
    meta_ref[:, row0:row0 + ts] = jnp.where(sub8 == 0, bucket, jnp.where(
        sub8 == 1, rank_hi, jnp.where(sub8 == 2, rank - RANK_SPLIT * rank_hi, 0.0)))
    run = run_ref[...] + jnp.sum(onehot, axis=1, keepdims=True)
    run_ref[...] = run
    cnt_ref[...] = jnp.broadcast_to(run, cnt_ref.shape)


def _route_io(mod, l, bsz, seq, d, ts, g, rw, rb):
    assert bsz * seq <= RANK_SPLIT * RANK_SPLIT
    nch = d // LANES
    nt = seq // ts
    tp = ts // ROUTE_PARTS
    row = lambda i: pl.BlockSpec((None, 1, d), lambda b, s, i=i: ((l * bsz + b) * 6 + i, 0, 0))
    const = lambda shape: pl.BlockSpec(shape, lambda b, s: (0, 0))
    in_specs = [const((1, d)), row(3), row(4), const((LANES, d)), const((LANES, 1)), const((tp, tp))]
    inputs = [g.reshape(1, d), mod, mod, rw.T, rb.reshape(LANES, 1), jnp.triu(jnp.ones((tp, tp), BF16), 1)]
    out_specs = [pl.BlockSpec((ts * nch, LANES), lambda b, s: (b * nt + s, 0)),
                 pl.BlockSpec((SUBLANES, ts), lambda b, s: (0, b * nt + s)), const((LANES, LANES))]
    out_shape = [jax.ShapeDtypeStruct((bsz * seq * nch, LANES), F32), jax.ShapeDtypeStruct((SUBLANES, bsz * seq), F32),
                 jax.ShapeDtypeStruct((LANES, LANES), F32)]
    return in_specs, inputs, out_specs, out_shape, pltpu.VMEM((LANES, 1), F32)


def _mod_kernel(c_ref, w_ref, b_ref, o_ref):
    c = c_ref[...]
    ca = c * jax.nn.sigmoid(c)
    o_ref[...] = jnp.dot(ca, w_ref[...], preferred_element_type=F32,
                         precision=lax.Precision.HIGHEST) + b_ref[...]


def _modulation(c, w, b, bn=1024):
    nl, d, n = w.shape
    bsz = c.shape[0]
    return pl.pallas_call(
        _mod_kernel,
        grid=(nl, n // bn),
        in_specs=[pl.BlockSpec((bsz, d), lambda l, j: (0, 0)),
                  pl.BlockSpec((None, d, bn), lambda l, j: (l, 0, j)),
                  pl.BlockSpec((None, 1, bn), lambda l, j: (l, 0, j))],
        out_specs=pl.BlockSpec((None, bsz, bn), lambda l, j: (l, 0, j)),
        out_shape=jax.ShapeDtypeStruct((nl, bsz, n), F32),
        compiler_params=_params(("arbitrary", "arbitrary")),
        name="modulation",
    )(c, w, b.reshape(nl, 1, n))


def _conv_kernel(x_ref, g_ref, sh_ref, sc_ref, gt_ref, win_ref, cw_ref, cb_ref, wout_ref, *rest, d, ts):
    route_in, (o_ref, hlin_ref, meta_ref, cnt_ref, carry_ref, run_ref) = rest[:6], rest[6:]

    @pl.when(pl.program_id(1) == 0)
    def _():
        carry_ref[...] = jnp.zeros_like(carry_ref)

    _route_init(run_ref)
    tp = ts // ROUTE_PARTS
    cw = cw_ref[...]
    row = lax.broadcasted_iota(jnp.int32, (tp, 1), 0)
    prev = carry_ref[...]
    for part in range(ROUTE_PARTS):
        rows = slice(part * tp, (part + 1) * tp)
        x = x_ref[rows, :]
        h = _norm_mod(x, g_ref[...], sh_ref[...], sc_ref[...]).astype(BF16)
        c_gate = jnp.dot(h, win_ref[:, d:2 * d], preferred_element_type=F32)
        v = jnp.dot(h, win_ref[:, 2 * d:3 * d], preferred_element_type=F32)
        z = c_gate * v
        z1 = jnp.where(row == 0, prev[SUBLANES - 1:SUBLANES], pltpu.roll(z, 1, 0))
        z2 = jnp.where(row == 0, prev[SUBLANES - 2:SUBLANES - 1],
                       jnp.where(row == 1, prev[SUBLANES - 1:SUBLANES], pltpu.roll(z, 2, 0)))
        prev = z[tp - SUBLANES:tp]
        zc = cw[0:1] * z2 + cw[1:2] * z1 + cw[2:3] * z + cb_ref[...]
        b_gate = jnp.dot(h, win_ref[:, 0:d], preferred_element_type=F32)
        y = jnp.dot((b_gate * zc).astype(BF16), wout_ref[...], preferred_element_type=F32)
        xn = x + gt_ref[...] * y
        o_ref[rows, :] = xn
        _route_tile(xn, part * tp, *route_in, hlin_ref, meta_ref, cnt_ref, run_ref)
    carry_ref[...] = prev


def _conv_layer(x, mod, l, g, w_in, cw, cb, w_out, route_args, ts=512):
    bsz, seq, d = x.shape
    row = lambda i: pl.BlockSpec((None, 1, d), lambda b, s, i=i: ((l * bsz + b) * 6 + i, 0, 0))
    const2 = lambda shape: pl.BlockSpec(shape, lambda b, s: (0, 0))
    xspec = pl.BlockSpec((None, ts, d), lambda b, s: (b, s, 0))
    r_in, r_args, r_out, r_shape, r_scratch = _route_io(mod, l, bsz, seq, d, ts, *route_args)
    return pl.pallas_call(
        functools.partial(_conv_kernel, d=d, ts=ts),
        grid=(bsz, seq // ts),
        in_specs=[xspec, const2((1, d)), row(0), row(1), row(2), const2((d, 3 * d)),
                  const2((SUBLANES, d)), const2((1, d)), const2((d, d))] + r_in,
        out_specs=[xspec] + r_out,
        out_shape=[jax.ShapeDtypeStruct(x.shape, F32)] + r_shape,
        scratch_shapes=[pltpu.VMEM((SUBLANES, d), F32), r_scratch],
        compiler_params=_params(("arbitrary", "arbitrary")),
        name="conv_layer",
    )(x, g.reshape(1, d), mod, mod, mod, w_in, jnp.pad(cw, ((0, SUBLANES - CONV_WIDTH), (0, 0))),
      cb.reshape(1, d), w_out, *r_args)


def _proj_kernel(x_ref, g_ref, sh_ref, sc_ref, w_ref, *o_refs, d, scale):
    h = _norm_mod(x_ref[...], g_ref[...], sh_ref[...], sc_ref[...]).astype(BF16)
    for j, o_ref in enumerate(o_refs):
        y = jnp.dot(h, w_ref[:, j * d:(j + 1) * d], preferred_element_type=F32)
        o_ref[...] = (y * scale).astype(o_ref.dtype)


def _proj(x, g, sh, sc, w, scale, ts=1024):
    bsz, seq, d = x.shape
    n_out = w.shape[1] // d
    xspec = pl.BlockSpec((None, ts, d), lambda b, s: (b, s, 0))
    row = pl.BlockSpec((None, 1, d), lambda b, s: (b, 0, 0))
    return pl.pallas_call(
        functools.partial(_proj_kernel, d=d, scale=scale),
        grid=(bsz, seq // ts),
        in_specs=[xspec, pl.BlockSpec((1, d), lambda b, s: (0, 0)), row, row,
                  pl.BlockSpec(w.shape, lambda b, s: (0, 0))],
        out_specs=[xspec] * n_out,
        out_shape=[jax.ShapeDtypeStruct(x.shape, BF16)] * n_out,
        compiler_params=_params(("arbitrary", "arbitrary")),
        name="proj",
    )(x, g.reshape(1, d), sh, sc, w)


def _flash_kernel(slope_ref, q_ref, k_ref, v_ref, lq1_ref, lk1_ref, lq2_ref, lk2_ref, g_ref,
                  o_ref, qq_ref, kk_ref, vv_ref, s0_ref, s1_ref, m_ref, acc_ref, *, tq, tk, rc, hd, lambda_init):
    h = pl.program_id(1)
    g = pl.program_id(2)
    seq, vd = k_ref.shape

    @pl.when((pl.program_id(0) == 0) & (h == 0) & (g == 0))
    def _():
        pos = lax.broadcasted_iota(jnp.int32, (seq, vd), 0)
        col = lax.broadcasted_iota(jnp.int32, (seq, vd), 1)
        lo = pos % POS_SPLIT
        kk_ref[:, vd:2 * vd] = jnp.where(col < 3, pos - lo, jnp.where(col < 6, lo, 0)).astype(F32).astype(BF16)
        vv_ref[:, vd:2 * vd] = jnp.ones((seq, vd), BF16)

    @pl.when(g == 0)
    def _():
        kk_ref[:, 0:vd] = k_ref[...]
        vv_ref[:, 0:vd] = v_ref[...]

    lane = lax.broadcasted_iota(jnp.int32, (tq, vd), 1)
    c0 = jnp.full((tq, vd), slope_ref[h] * LOG2E, F32)
    c1 = c0.astype(BF16).astype(F32)
    c2 = (c0 - c1).astype(BF16).astype(F32)
    c3 = c0 - c1 - c2
    piece = lane % 3
    slope_cols = jnp.where(lane < 6, jnp.where(piece == 0, c1, jnp.where(piece == 1, c2, c3)), 0.0).astype(BF16)
    for u in range(2):
        q = q_ref[u * tq:(u + 1) * tq, :]
        zero = jnp.zeros_like(q)
        qq_ref[u, 0:tq, 0:vd] = jnp.where(lane < hd, q, zero)
        qq_ref[u, tq:2 * tq, 0:vd] = jnp.where(lane >= hd, q, zero)
        qq_ref[u, 0:tq, vd:2 * vd] = slope_cols
        qq_ref[u, tq:2 * tq, vd:2 * vd] = slope_cols
    m_ref[...] = jnp.full_like(m_ref, NEG_INF)
    acc_ref[...] = jnp.zeros_like(acc_ref)

    def scores(u, j, s_ref):
        kj = kk_ref[pl.ds(pl.multiple_of(j * tk, tk), tk), :]
        s_ref[...] = lax.dot_general(qq_ref[u], kj, (((1,), (1,)), ((), ())), preferred_element_type=F32)

    def update(u, j, s_ref, masked):
        k0 = pl.multiple_of(j * tk, tk)
        rcu = rc // 2 if masked else rc
        for c in range(2 * tq // rcu):
            rows = pl.ds(c * rcu, rcu)
            r0 = (c * rcu) % tq
            nk = min(tk, r0 + rcu) if masked else tk
            s = s_ref[rows, 0:nk]
            if masked:
                qpos = lax.broadcasted_iota(jnp.int32, (rcu, nk), 0) + r0
                kpos = lax.broadcasted_iota(jnp.int32, (rcu, nk), 1)
                s = jnp.where(qpos >= kpos, s, NEG_INF)
            m_prev = m_ref[u, rows, :]
            m_new = jnp.maximum(m_prev, jnp.max(s, axis=-1, keepdims=True))
            alpha = jnp.exp2(m_prev - m_new)
            p = jnp.exp2(s - jnp.tile(m_new, (1, nk // LANES)))
            pv = jnp.dot(p.astype(BF16), vv_ref[pl.ds(k0, nk), :], preferred_element_type=F32)
            acc_ref[u, rows, :] = jnp.tile(alpha, (1, 2 * vd // LANES)) * acc_ref[u, rows, :] + pv
            m_ref[u, rows, :] = m_new

    def finalize(u):
        lam = (jnp.exp(jnp.sum(lq1_ref[...] * lk1_ref[...], keepdims=True))
               - jnp.exp(jnp.sum(lq2_ref[...] * lk2_ref[...], keepdims=True)) + lambda_init)
        acc = acc_ref[u]
        o12 = acc[:, 0:vd] / acc[:, vd:2 * vd]
        o = o12[0:tq] - lam * o12[tq:2 * tq]
        on = o * lax.rsqrt(jnp.mean(o * o, axis=-1, keepdims=True) + EPS) * g_ref[...]
        o_ref[u * tq:(u + 1) * tq, :] = (on * (1.0 - lambda_init)).astype(o_ref.dtype)

    def pairs(u, sa_ref, sb_ref):
        def body(i, carry):
            j = 2 * i
            scores(u, j + 1, sb_ref)
            update(u, j, sa_ref, False)
            scores(u, j + 2, sa_ref)
            update(u, j + 1, sb_ref, False)
            return carry
        lax.fori_loop(0, g, body, 0)

    scores(0, 0, s0_ref)
    pairs(0, s0_ref, s1_ref)
    scores(1, 0, s1_ref)
    update(0, 2 * g, s0_ref, True)
    pairs(1, s1_ref, s0_ref)
    scores(1, 2 * g + 1, s0_ref)
    update(1, 2 * g, s1_ref, False)
    finalize(0)
    update(1, 2 * g + 1, s0_ref, True)
    finalize(1)


def _diff_attention(q, k, v, lq1, lk1, lq2, lk2, subln_g, lambda_init, tq=512, tk=512, rc=512):
    bsz, seq, d = q.shape
    vd = d // N_HEADS
    hd = vd // 2
    assert vd == LANES and seq <= POS_SPLIT * 256 and tq == tk and seq % (2 * tq) == 0
    slopes = jnp.exp2(-8.0 * (jnp.arange(N_HEADS, dtype=F32) + 1.0) / N_HEADS)
    vec = lambda n: pl.BlockSpec((1, n), lambda b, h, i, sl: (0, 0))
    kvspec = pl.BlockSpec((None, seq, vd), lambda b, h, i, sl: (b, 0, h))
    qspec = pl.BlockSpec((None, 2 * tq, vd), lambda b, h, i, sl: (b, i, h))
    grid_spec = pltpu.PrefetchScalarGridSpec(
        num_scalar_prefetch=1,
        grid=(bsz, N_HEADS, seq // (2 * tq)),
        in_specs=[qspec, kvspec, kvspec, vec(hd), vec(hd), vec(hd), vec(hd), vec(vd)],
        out_specs=qspec,
        scratch_shapes=[pltpu.VMEM((2, 2 * tq, 2 * vd), BF16), pltpu.VMEM((seq, 2 * vd), BF16),
                        pltpu.VMEM((seq, 2 * vd), BF16), pltpu.VMEM((2 * tq, tk), F32), pltpu.VMEM((2 * tq, tk), F32),
                        pltpu.VMEM((2, 2 * tq, LANES), F32),
                        pltpu.VMEM((2, 2 * tq, 2 * vd), F32)],
    )
    return pl.pallas_call(
        functools.partial(_flash_kernel, tq=tq, tk=tk, rc=rc, hd=hd, lambda_init=lambda_init),
        grid_spec=grid_spec,
        out_shape=jax.ShapeDtypeStruct(q.shape, BF16),
        compiler_params=_params(("arbitrary", "arbitrary", "arbitrary")),
        name="diff_attention",
    )(slopes, q, k, v, lq1.reshape(1, hd), lk1.reshape(1, hd), lq2.reshape(1, hd), lk2.reshape(1, hd),
      subln_g.reshape(1, vd))


def _oproj_kernel(a_ref, w_ref, x_ref, gt_ref, *rest):
    route_in, (o_ref, hlin_ref, meta_ref, cnt_ref, run_ref) = rest[:6], rest[6:]
    _route_init(run_ref)
    tp = x_ref.shape[0] // ROUTE_PARTS
    for part in range(ROUTE_PARTS):
        rows = slice(part * tp, (part + 1) * tp)
        y = jnp.dot(a_ref[rows, :], w_ref[...], preferred_element_type=F32)
        xn = x_ref[rows, :] + gt_ref[...] * y
        o_ref[rows, :] = xn
        _route_tile(xn, part * tp, *route_in, hlin_ref, meta_ref, cnt_ref, run_ref)


def _oproj(a, w, x, mod, l, route_args, ts=1024):
    bsz, seq, d = x.shape
    xspec = pl.BlockSpec((None, ts, d), lambda b, s: (b, s, 0))
    r_in, r_args, r_out, r_shape, r_scratch = _route_io(mod, l, bsz, seq, d, ts, *route_args)
    return pl.pallas_call(
        _oproj_kernel,
        grid=(bsz, seq // ts),
        in_specs=[xspec, pl.BlockSpec((d, d), lambda b, s: (0, 0)), xspec,
                  pl.BlockSpec((None, 1, d), lambda b, s: ((l * bsz + b) * 6 + 2, 0, 0))] + r_in,
        out_specs=[xspec] + r_out,
        out_shape=[jax.ShapeDtypeStruct(x.shape, F32)] + r_shape,
        scratch_shapes=[r_scratch],
        compiler_params=_params(("arbitrary", "arbitrary")),
        name="attn_out",
    )(a, w, x, mod, *r_args)


def _row_dma_loop(n_rows, copy_fn):
    def body(i, carry):
        for u in range(DMA_UNROLL):
            copy_fn(i * DMA_UNROLL + u).start(priority=u % 2)
        return carry
    lax.fori_loop(0, n_rows // DMA_UNROLL, body, 0)


def _dispatch_kernel(pos_ref, ztile_ref, hlin_ref, hs_ref, zero_ref, sem, zsem, *, nch, chunk, tm):
    def rows(ref, r, n=1):
        return ref.at[pl.ds(pl.multiple_of(r * nch, nch), n * nch)]

    c = pl.program_id(0)

    @pl.when(c == 0)
    def _():
        zero_ref[...] = jnp.zeros_like(zero_ref)
        fill = lambda k: pltpu.make_async_copy(zero_ref, rows(hs_ref, ztile_ref[k] * tm, tm), zsem)
        for k in range(ztile_ref.shape[0]):
            @pl.when(ztile_ref[k] >= 0)
            def _(k=k):
                fill(k).start()
        for k in range(ztile_ref.shape[0]):
            @pl.when(ztile_ref[k] >= 0)
            def _(k=k):
                fill(k).wait()

    _row_dma_loop(chunk, lambda r: pltpu.make_async_copy(
        rows(hlin_ref, r), rows(hs_ref, pos_ref[c * chunk + r]), sem))
    pltpu.make_async_copy(hlin_ref, rows(hs_ref, 0, chunk), sem).wait()


def _dispatch(pos, ztile, hlin, n_rows, nch, tm, chunk=2048):
    t_all = pos.shape[0]
    grid_spec = pltpu.PrefetchScalarGridSpec(
        num_scalar_prefetch=2, grid=(t_all // chunk,),
        in_specs=[pl.BlockSpec((chunk * nch, LANES), lambda c, pos, zt: (c, 0))],
        out_specs=pl.BlockSpec(memory_space=pl.ANY),
        scratch_shapes=[pltpu.VMEM((tm * nch, LANES), F32), pltpu.SemaphoreType.DMA(()),
                        pltpu.SemaphoreType.DMA(())])
    return pl.pallas_call(
        functools.partial(_dispatch_kernel, nch=nch, chunk=chunk, tm=tm),
        grid_spec=grid_spec,
        out_shape=jax.ShapeDtypeStruct((n_rows * nch, LANES), F32),
        compiler_params=_params(("arbitrary",)),
        name="moe_dispatch",
    )(pos, ztile, hlin)


def _ffn_kernel(ta_ref, tb_ref, tg_ref, tv_ref, tblk_ref, tnew_ref, hs_ref, rw_ref, rb_ref, *rest, tm, nch):
    del tblk_ref
    w_f32, ys_ref, w_bf16 = rest[:6], rest[6], rest[7:]
    w1a_ref, w3a_ref, w2a_ref, w1b_ref, w3b_ref, w2b_ref = w_bf16
    i = pl.program_id(0)

    @pl.when(tv_ref[i] == 0)
    def _():
        ys_ref[...] = jnp.zeros_like(ys_ref)

    @pl.when(tnew_ref[i] == 1)
    def _():
        for src, dst in zip(w_f32, w_bf16):
            dst[...] = src[...].astype(BF16)

    @pl.when(tv_ref[i] == 1)
    def _():
        x = jnp.concatenate([hs_ref[pl.ds(j, tm, stride=nch), :] for j in range(nch)], axis=-1).astype(BF16)
        logits = jnp.dot(x, rw_ref[...], preferred_element_type=F32) + rb_ref[...]
        lane = lax.broadcasted_iota(jnp.int32, logits.shape, 1)
        pick = lambda col, val: jnp.sum(jnp.where(lane == col, val, 0.0), axis=-1, keepdims=True)
        gl = jnp.where(lane < N_GROUPS, logits, NEG_INF)
        eg = jnp.exp(gl - jnp.max(gl, axis=-1, keepdims=True))
        pg = pick(tg_ref[i], eg) / jnp.sum(eg, axis=-1, keepdims=True)
        la = pick(N_GROUPS + ta_ref[i], logits)
        lb = pick(N_GROUPS + tb_ref[i], logits)
        mx = jnp.maximum(la, lb)
        ea = jnp.exp(la - mx)
        eb = jnp.exp(lb - mx)
        wa = pg * (ea / (ea + eb))
        wb = pg * (eb / (ea + eb))

        def ffn(w1_ref, w3_ref, w2_ref):
            a = jnp.dot(x, w1_ref[...], preferred_element_type=F32)
            b = jnp.dot(x, w3_ref[...], preferred_element_type=F32)
            hid = (a * jax.nn.sigmoid(a)) * b
            return jnp.dot(hid.astype(BF16), w2_ref[...], preferred_element_type=F32)

        y = wa * ffn(w1a_ref, w3a_ref, w2a_ref) + wb * ffn(w1b_ref, w3b_ref, w2b_ref)
        for j in range(nch):
            ys_ref[pl.ds(j, tm, stride=nch), :] = y[:, j * LANES:(j + 1) * LANES]


def _ffn(tiles, hs, rw, rb, l, w1, w3, w2, tm, nch):
    ta = tiles[0]
    n_tiles = ta.shape[0]
    d, de = w1.shape[2:]
    rows = pl.BlockSpec((tm * nch, LANES), lambda i, ta, tb, tg, tv, tblk, tnew: (tblk[i], 0))
    const = lambda shape: pl.BlockSpec(shape, lambda i, *_: (0, 0))
    wa = lambda shape: pl.BlockSpec((None, None) + shape, lambda i, ta, tb, tg, tv, tblk, tnew: (l, ta[i], 0, 0))
    wb = lambda shape: pl.BlockSpec((None, None) + shape, lambda i, ta, tb, tg, tv, tblk, tnew: (l, tb[i], 0, 0))
    grid_spec = pltpu.PrefetchScalarGridSpec(
        num_scalar_prefetch=6, grid=(n_tiles,),
        in_specs=[rows, const((d, LANES)), const((1, LANES)),
                  wa((d, de)), wa((d, de)), wa((de, d)), wb((d, de)), wb((d, de)), wb((de, d))],
        out_specs=pl.BlockSpec((tm * nch, LANES), lambda i, *_: (i, 0)),
        scratch_shapes=[pltpu.VMEM(s, BF16) for s in ((d, de), (d, de), (de, d)) * 2])
    return pl.pallas_call(
        functools.partial(_ffn_kernel, tm=tm, nch=nch),
        grid_spec=grid_spec,
        out_shape=jax.ShapeDtypeStruct(hs.shape, F32),
        compiler_params=pltpu.CompilerParams(dimension_semantics=("arbitrary",), vmem_limit_bytes=FFN_VMEM_LIMIT),
        name="moe_ffn",
    )(*tiles, hs, rw, rb, w1, w3, w2, w1, w3, w2)


def _combine_kernel(pos_ref, ys_ref, x_ref, gt_ref, fg_ref, o_ref, buf_ref, sem, *, tg, nch, final):
    i = pl.program_id(0)
    n = pl.num_programs(0)

    def rows(ref, r, k=1):
        return ref.at[pl.ds(pl.multiple_of(r * nch, nch), k * nch)]

    def fetch(tile, slot):
        _row_dma_loop(tg, lambda r: pltpu.make_async_copy(
            rows(ys_ref, pos_ref[tile * tg + r]), rows(buf_ref, slot * tg + r), sem.at[slot]))

    @pl.when(i == 0)
    def _():
        fetch(0, 0)

    @pl.when(i + 1 < n)
    def _():
        fetch(i + 1, (i + 1) % 2)

    slot = i % 2
    pltpu.make_async_copy(rows(ys_ref, 0, tg), rows(buf_ref, slot * tg, tg), sem.at[slot]).wait()
    base = slot * (tg * nch)
    y = jnp.concatenate([buf_ref[pl.ds(base + j, tg, stride=nch), :] for j in range(nch)], axis=-1)
    xn = x_ref[...] + gt_ref[...] * y
    if final:
        xn = (xn * lax.rsqrt(jnp.mean(xn * xn, axis=-1, keepdims=True) + EPS)) * fg_ref[...]
    o_ref[...] = xn


def _combine(pos, ys, x2, mod, l, bsz, seq, final_g, final, nch, tg=512):
    t_all, d = x2.shape
    tile = pl.BlockSpec((tg, d), lambda t, pos: (t, 0))
    grid_spec = pltpu.PrefetchScalarGridSpec(
        num_scalar_prefetch=1, grid=(t_all // tg,),
        in_specs=[pl.BlockSpec(memory_space=pl.ANY), tile,
                  pl.BlockSpec((None, 1, d), lambda t, pos: ((l * bsz + (t * tg) // seq) * 6 + 5, 0, 0)),
                  pl.BlockSpec((1, d), lambda t, pos: (0, 0))],
        out_specs=tile,
        scratch_shapes=[pltpu.VMEM((2 * tg * nch, LANES), F32), pltpu.SemaphoreType.DMA((2,))])
    return pl.pallas_call(
        functools.partial(_combine_kernel, tg=tg, nch=nch, final=final),
        grid_spec=grid_spec,
        out_shape=jax.ShapeDtypeStruct(x2.shape, F32),
        compiler_params=_params(("arbitrary",)),
        name="moe_combine",
    )(pos, ys, x2, mod, final_g.reshape(1, d))


def _moe(x, routed, mod, l, rw, rb, w1, w3, w2, final_g, final, tm=512):
    bsz, seq, d = x.shape
    t_all = bsz * seq
    nch = d // LANES
    x2 = x.reshape(t_all, d)
    hlin, meta, cnt = routed

    n_tiles = t_all // tm + N_BUCKETS
    bucket = meta[0].astype(jnp.int32)
    rank = (meta[1] * RANK_SPLIT + meta[2]).astype(jnp.int32)
    count = cnt[:N_BUCKETS, 0].astype(jnp.int32)
    padded = (count + tm - 1) // tm * tm
    ends = jnp.cumsum(padded)
    starts = ends - padded
    buckets = jnp.arange(N_BUCKETS, dtype=jnp.int32)
    pos = rank + jnp.sum(jnp.where(bucket[:, None] == buckets, starts, 0), axis=-1)
    n_valid = ends[-1] // tm
    tile_id = jnp.arange(n_tiles, dtype=jnp.int32)
    tblk = jnp.minimum(tile_id, n_valid - 1)
    tbucket = jnp.sum((ends <= (tblk * tm)[:, None]).astype(jnp.int32), axis=-1)
    pair = tbucket % N_PAIRS
    tgrp = tbucket // N_PAIRS
    pick = lambda table: jnp.sum(jnp.where(pair[:, None] == jnp.arange(N_PAIRS), jnp.asarray(table, jnp.int32), 0), -1)
    ta = tgrp * EXPERTS_PER_GROUP + pick(PAIR_LO)
    tb = tgrp * EXPERTS_PER_GROUP + pick(PAIR_HI)
    tv = (tile_id < n_valid).astype(jnp.int32)
    tnew = jnp.concatenate([jnp.ones((1,), jnp.int32), (tbucket[1:] != tbucket[:-1]).astype(jnp.int32)])
    last = jnp.where(padded > 0, ends // tm - 1, -1)
    tail = jnp.where(n_valid + buckets < n_tiles, n_valid + buckets, -1)
    ztile = jnp.concatenate([last, tail]).astype(jnp.int32)

    hs = _dispatch(pos, ztile, hlin, n_tiles * tm, nch, tm)
    ys = _ffn((ta, tb, tgrp, tv, tblk, tnew), hs, rw, rb, l, w1, w3, w2, tm, nch)
    out = _combine(pos, ys, x2, mod, l, bsz, seq, final_g, final, nch)
    return out.reshape(bsz, seq, d)


def kernel(x, c, mod_w, mod_b, norm_mix_g, norm_ffn_g, conv_in_w, conv_w, conv_b, conv_out_w, kv_mod_w, kv_mod_b, kv_norm_g, kv_w, q_w, lam_q1, lam_k1, lam_q2, lam_k2, subln_g, o_w, router_group_w, router_group_b, router_exp_w, router_exp_b, exp_w1, exp_w3, exp_w2, final_norm_g):
    bsz, seq, d = x.shape
    depth = mod_w.shape[0]
    n_a = conv_in_w.shape[0]
    hd = d // (2 * N_HEADS)

    mod = _modulation(c, mod_w, mod_b).reshape(depth * bsz * 6, 1, d)
    kvm = _modulation(c, kv_mod_w[None], kv_mod_b[None]).reshape(bsz, 2, 1, d)
    kv_sh, kv_sc = kvm[:, 0], kvm[:, 1]

    pad = LANES - N_GROUPS - N_EXPERTS
    rw = jnp.pad(jnp.concatenate([router_group_w, router_exp_w], axis=-1), ((0, 0), (0, 0), (0, pad))).astype(BF16)
    rb = jnp.pad(jnp.concatenate([router_group_b, router_exp_b], axis=-1), ((0, 0), (0, pad)))[:, None, :]

    k = v = None
    for l in range(depth):
        if l == n_a:
            k, v = _proj(x, kv_norm_g, kv_sh, kv_sc, kv_w.astype(BF16), 1.0)
        route_args = (norm_ffn_g[l], rw[l], rb[l])
        if l < n_a:
            x, *routed = _conv_layer(x, mod, l, norm_mix_g[l], conv_in_w[l].astype(BF16), conv_w[l], conv_b[l],
                                     conv_out_w[l].astype(BF16), route_args)
        else:
            j = l - n_a
            lambda_init = 0.8 - 0.6 * float(np.exp(-0.3 * l))
            m4 = mod.reshape(depth, bsz, 6, 1, d)
            (q,) = _proj(x, norm_mix_g[l], m4[l, :, 0], m4[l, :, 1], q_w[j].astype(BF16), hd ** -0.5 * LOG2E)
            a = _diff_attention(q, k, v, lam_q1[j], lam_k1[j], lam_q2[j], lam_k2[j], subln_g[j], lambda_init)
            x, *routed = _oproj(a, o_w[j].astype(BF16), x, mod, l, route_args)
        x = _moe(x, routed, mod, l, rw[l], rb[l], exp_w1, exp_w3, exp_w2, final_norm_g, final=(l == depth - 1))
    return x
```
